```python
import math
import jax, jax.numpy as jnp
from jax import lax
import numpy as np

D_MODEL = 1024
BATCH = 4
SEQ = 4096
DEPTH = 2

CTX_LEN = 256
GRID_W = 64

SSD_D_INNER = 2 * D_MODEL
SSD_HEADDIM = 64
SSD_HEADS = SSD_D_INNER // SSD_HEADDIM
SSD_GROUPS = 8
SSD_HPG = SSD_HEADS // SSD_GROUPS
SSD_STATE = 128
SSD_CONV = 5
SSD_CHUNK = 128
SSD_GN = SSD_GROUPS * SSD_STATE
XBC_WIDTH = SSD_D_INNER + 2 * SSD_GN

POOL_WIDTH = D_MODEL
POOL_WINDOWS = (2, 4, 8, 16)
POOL_GROUP = POOL_WIDTH // len(POOL_WINDOWS)

FFN_HIDDEN = -(-(8 * D_MODEL) // (3 * 256)) * 256

IN_COLS = SSD_D_INNER + XBC_WIDTH + 2 * SSD_HEADS + POOL_WIDTH + 2 * D_MODEL
IN_SPLITS = (SSD_D_INNER,
             SSD_D_INNER + XBC_WIDTH,
             SSD_D_INNER + XBC_WIDTH + 2 * SSD_HEADS,
             SSD_D_INNER + XBC_WIDTH + 2 * SSD_HEADS + POOL_WIDTH)
EPS = 1e-6

kernel_name = "hybrid_ssd_pool_dit_block"


def rmsnorm(x, g):
    xf = x.astype(jnp.float32)
    y = xf * lax.rsqrt(jnp.mean(xf * xf, axis=-1, keepdims=True) + EPS)
    return (y * g).astype(x.dtype)


def modulate(h, shift, scale):
    return h * (1.0 + scale) + shift


def adaln(cond, w, b):
    m = jax.nn.silu(cond) @ w + b
    return jnp.split(m[:, None, :], 6, axis=-1)


def dwconv_centred(u, w, b):
    k = w.shape[0]
    pad = k // 2
    L = u.shape[1]
    up = jnp.pad(u, ((0, 0), (pad, pad), (0, 0)))
    out = b
    for i in range(k):
        out = out + w[i] * up[:, i:i + L]
    return out


def pool_minus_self(u):
    L = u.shape[-2]
    uf = u.astype(jnp.float32)
    cs = jnp.cumsum(uf, axis=-2)
    cs = jnp.concatenate([jnp.zeros_like(cs[..., :1, :]), cs], axis=-2)
    t = jnp.arange(L)
    outs = []
    for gi, k in enumerate(POOL_WINDOWS):
        lo = jnp.clip(t - k // 2, 0, L)
        hi = jnp.clip(t + k // 2, 0, L)
        seg = cs[..., gi * POOL_GROUP:(gi + 1) * POOL_GROUP]
        s = jnp.take(seg, hi, axis=-2) - jnp.take(seg, lo, axis=-2)
        outs.append(s / (hi - lo).astype(jnp.float32)[:, None])
    mean = jnp.concatenate(outs, axis=-1)
    return (mean - uf).astype(u.dtype)


def ssd_scan(xh, dt, a_log, bm, cm, init_state, return_y):
    Bsz, L, H, P = xh.shape
    G, N = bm.shape[2], bm.shape[3]
    R = H // G
    Q = SSD_CHUNK
    nc = L // Q
    A = -jnp.exp(a_log.astype(jnp.float32))
    acum = jnp.cumsum((dt * A).reshape(Bsz, nc, Q, H), axis=2)
    xc = (xh.astype(jnp.float32) * dt[..., None]).reshape(Bsz, nc, Q, G, R, P)
    bc = bm.astype(jnp.float32).reshape(Bsz, nc, Q, G, N)
    cc = cm.astype(jnp.float32).reshape(Bsz, nc, Q, G, N)
    a_last = acum[:, :, -1]
    decay_to_end = jnp.exp(a_last[:, :, None, :] - acum).reshape(Bsz, nc, Q, G, R)
    chunk_states = jnp.einsum('bcjgn,bcjgr,bcjgrp->bcgrpn', bc, decay_to_end, xc)

    def step(state, inp):
        cs, al = inp
        return state * jnp.exp(al)[..., None, None] + cs, state

    final, entering = lax.scan(
        step, init_state.reshape(Bsz, G, R, P, N),
        (jnp.moveaxis(chunk_states, 1, 0), jnp.moveaxis(a_last.reshape(Bsz, nc, G, R), 1, 0)))
    final = final.reshape(Bsz, H, P, N)
    if not return_y:
        return None, final
    entering = jnp.moveaxis(entering, 0, 1)
    mask = jnp.tril(jnp.ones((Q, Q), dtype=bool))[None, None, :, :, None]
    diff = acum[:, :, :, None, :] - acum[:, :, None, :, :]
    decay = jnp.exp(jnp.where(mask, diff, -jnp.inf)).reshape(Bsz, nc, Q, Q, G, R)
    cb = jnp.einsum('bcign,bcjgn->bcijg', cc, bc)
    y_intra = jnp.einsum('bcijgr,bcjgrp->bcigrp', cb[..., None] * decay, xc)
    y_inter = (jnp.einsum('bcign,bcgrpn->bcigrp', cc, entering)
               * jnp.exp(acum).reshape(Bsz, nc, Q, G, R)[..., None])
    y = (y_intra + y_inter).reshape(Bsz, L, H, P)
    return y.astype(xh.dtype), final


def ssd_bidir(xs, dt, bm, cm, a_log, d_skip, init_states, return_y):
    ys = []
    finals = []
    for d in range(2):
        flip = (lambda a: jnp.flip(a, axis=1)) if d == 1 else (lambda a: a)
        y, s = ssd_scan(flip(xs), flip(dt[:, :, d]), a_log[d], flip(bm), flip(cm), init_states[d], return_y)
        finals.append(s)
        if return_y:
            ys.append(flip(y) + d_skip[d][:, None] * xs)
    y = (ys[0] + ys[1]) if return_y else None
    return y, finals


def prepare(h, w_in, conv_w, conv_b, dt_bias):
    Bsz, L = h.shape[0], h.shape[1]
    proj = h @ w_in
    z, xbc, dt_raw, u_pool, gate_logits = jnp.split(proj, IN_SPLITS, axis=-1)
    xbc = jax.nn.silu(dwconv_centred(xbc, conv_w, conv_b))
    xs, bm, cm = jnp.split(xbc, (SSD_D_INNER, SSD_D_INNER + SSD_GN), axis=-1)
    xs = xs.reshape(Bsz, L, SSD_HEADS, SSD_HEADDIM)
    bm = bm.reshape(Bsz, L, SSD_GROUPS, SSD_STATE)
    cm = cm.reshape(Bsz, L, SSD_GROUPS, SSD_STATE)
    dt = jax.nn.softplus((dt_raw.reshape(Bsz, L, 2, SSD_HEADS) + dt_bias).astype(jnp.float32))
    return z, xs, bm, cm, dt, u_pool, gate_logits


def mixer_out(z, y, u_pool, gate_logits, ssd_norm_w, w_ssd_out, pool_w, pool_scale, w_pool_out, w_out, on_grid):
    Bsz, L = z.shape[0], z.shape[1]
    yz = y.reshape(Bsz, L, SSD_D_INNER) * jax.nn.silu(z)
    yn = rmsnorm(yz.reshape(Bsz, L, SSD_GROUPS, SSD_D_INNER // SSD_GROUPS), 1.0).reshape(Bsz, L, SSD_D_INNER)
    o_ssd = (yn * ssd_norm_w) @ w_ssd_out
    if on_grid:
        rows = L // GRID_W
        pm = pool_minus_self(u_pool.reshape(Bsz, rows, GRID_W, POOL_WIDTH)).reshape(Bsz, L, POOL_WIDTH)
    else:
        pm = pool_minus_self(u_pool)
    pm = jnp.einsum('blgi,gio->blgo', pm.reshape(Bsz, L, len(POOL_WINDOWS), POOL_GROUP), pool_w)
    o_pool = (pm.reshape(Bsz, L, POOL_WIDTH) * pool_scale) @ w_pool_out
    g_ssd, g_pool = jnp.split(jax.nn.sigmoid(gate_logits), 2, axis=-1)
    return (g_ssd * o_ssd + g_pool * o_pool) @ w_out


def swiglu(h, w_gate_up, w_down):
    a, b = jnp.split(h @ w_gate_up, 2, axis=-1)
    return (jax.nn.silu(a) * b) @ w_down


def setup_inputs(seed: int = 0) -> dict:
    key = jax.random.key(seed)
    ks = jax.random.split(key, 24)
    f32 = jnp.float32

    def nrm(k, shape, scale):
        return jax.random.normal(k, shape, f32) * scale

    H = SSD_HEADS
    dt0 = jnp.exp(jax.random.uniform(ks[10], (DEPTH, 2, H), f32, minval=math.log(1e-3), maxval=math.log(1e-1)))
    return {
        "x": nrm(ks[0], (BATCH, SEQ, D_MODEL), 1.0),
        "c": nrm(ks[1], (BATCH, D_MODEL), 1.0),
        "ctx": nrm(ks[2], (BATCH, CTX_LEN, D_MODEL), 1.0),
        "c_ctx": nrm(ks[3], (D_MODEL,), 1.0),
        "w_ada": nrm(ks[4], (DEPTH, D_MODEL, 6 * D_MODEL), 0.5 * D_MODEL ** -0.5),
        "b_ada": nrm(ks[5], (DEPTH, 6 * D_MODEL), 0.02),
        "g_mix": 1.0 + nrm(ks[6], (DEPTH, D_MODEL), 0.05),
        "w_in": nrm(ks[7], (DEPTH, D_MODEL, IN_COLS), D_MODEL ** -0.5),
        "conv_w": nrm(ks[8], (DEPTH, SSD_CONV, XBC_WIDTH), SSD_CONV ** -0.5),
        "conv_b": nrm(ks[9], (DEPTH, XBC_WIDTH), 0.02),
        "dt_bias": dt0 + jnp.log(-jnp.expm1(-dt0)),
        "a_log": jnp.log(jax.random.uniform(ks[11], (DEPTH, 2, H), f32, minval=1.0, maxval=16.0)),
        "d_skip": 1.0 + nrm(ks[12], (DEPTH, 2, H), 0.05),
        "ssd_norm_w": 1.0 + nrm(ks[13], (DEPTH, SSD_D_INNER), 0.05),
        "w_ssd_out": nrm(ks[14], (DEPTH, SSD_D_INNER, D_MODEL), SSD_D_INNER ** -0.5),
        "pool_w": nrm(ks[15], (DEPTH, len(POOL_WINDOWS), POOL_GROUP, POOL_GROUP), POOL_GROUP ** -0.5),
        "pool_scale": 1.0 + nrm(ks[16], (DEPTH, POOL_WIDTH), 0.1),
        "w_pool_out": nrm(ks[17], (DEPTH, POOL_WIDTH, D_MODEL), POOL_WIDTH ** -0.5),
        "w_out": nrm(ks[18], (DEPTH, D_MODEL, D_MODEL), D_MODEL ** -0.5),
        "g_ffn": 1.0 + nrm(ks[19], (DEPTH, D_MODEL), 0.05),
        "w_gate_up": nrm(ks[20], (DEPTH, D_MODEL, 2 * FFN_HIDDEN), D_MODEL ** -0.5),
        "w_down": nrm(ks[21], (DEPTH, FFN_HIDDEN, D_MODEL), FFN_HIDDEN ** -0.5),
        "g_final": 1.0 + nrm(ks[22], (D_MODEL,), 0.05),
    }


def reference(x, c, ctx, c_ctx, w_ada, b_ada, g_mix, w_in, conv_w, conv_b, dt_bias, a_log, d_skip,
              ssd_norm_w, w_ssd_out, pool_w, pool_scale, w_pool_out, w_out, g_ffn, w_gate_up, w_down,
              g_final):
    cx = ctx
    for l in range(DEPTH):
        last = l == DEPTH - 1
        sh1, sc1, ga1, sh2, sc2, ga2 = adaln(c, w_ada[l], b_ada[l])
        csh1, csc1, cga1, csh2, csc2, cga2 = adaln(c_ctx[None, :], w_ada[l], b_ada[l])

        h_lat = modulate(rmsnorm(x, g_mix[l]), sh1, sc1)
        h_ctx = modulate(rmsnorm(cx, g_mix[l]), csh1, csc1)
        zc, xsc, bmc, cmc, dtc, upc, glc = prepare(h_ctx, w_in[l], conv_w[l], conv_b[l], dt_bias[l])
        zl, xsl, bml, cml, dtl, upl, gll = prepare(h_lat, w_in[l], conv_w[l], conv_b[l], dt_bias[l])
        zero = jnp.zeros((xsc.shape[0], SSD_HEADS, SSD_HEADDIM, SSD_STATE), jnp.float32)
        y_ctx, ctx_states = ssd_bidir(xsc, dtc, bmc, cmc, a_log[l], d_skip[l], (zero, zero), not last)
        y_lat, _ = ssd_bidir(xsl, dtl, bml, cml, a_log[l], d_skip[l], ctx_states, True)
        x = x + ga1 * mixer_out(zl, y_lat, upl, gll, ssd_norm_w[l], w_ssd_out[l], pool_w[l], pool_scale[l],
                                w_pool_out[l], w_out[l], True)
        x = x + ga2 * swiglu(modulate(rmsnorm(x, g_ffn[l]), sh2, sc2), w_gate_up[l], w_down[l])

        if not last:
            cx = cx + cga1 * mixer_out(zc, y_ctx, upc, glc, ssd_norm_w[l], w_ssd_out[l], pool_w[l],
                                       pool_scale[l], w_pool_out[l], w_out[l], False)
            cx = cx + cga2 * swiglu(modulate(rmsnorm(cx, g_ffn[l]), csh2, csc2), w_gate_up[l], w_down[l])
    return rmsnorm(x, g_final)
```

```python
import functools

import numpy as np
import jax
import jax.numpy as jnp
from jax import lax
from jax.experimental import pallas as pl
from jax.experimental.pallas import tpu as pltpu

F32 = jnp.float32
BF = jnp.bfloat16

D = 1024
DEPTH = 2
H = 32
P = 64
G = 8
R = H // G
N = 128
Q = 128
DI = H * P
GN = G * N
GW = R * P
KC = 5
FH = 2816
POOL_WINDOWS = (2, 4, 8, 16)
PG = D // len(POOL_WINDOWS)
GRID_W = 64
EPS = 1e-6
DTW = 128

C_Z, C_X, C_B, C_C, C_POOL, C_GS, C_GP = 0, 2048, 4096, 5120, 6144, 7168, 8192
MAIN_W = 9216

VMEM_LIMIT = 56 * 1024 * 1024


def _dot(a, b):
    return jnp.dot(a, b, preferred_element_type=F32)


def _split3(a):
    hi = a.astype(BF)
    r = a - hi.astype(F32)
    mid = r.astype(BF)
    lo = (r - mid.astype(F32)).astype(BF)
    return hi, mid, lo


def _dot3_rhs(m, a):
    hi, mid, lo = _split3(a)
    return (_dot(m, lo) + _dot(m, mid)) + _dot(m, hi)


def _dot3_lhs(a, m):
    hi, mid, lo = _split3(a)
    return (_dot(lo, m) + _dot(mid, m)) + _dot(hi, m)


def _silu(v):
    return v * jax.nn.sigmoid(v)


def _params(sem):
    return pltpu.CompilerParams(dimension_semantics=sem, vmem_limit_bytes=VMEM_LIMIT)


def _mod_spec(l, k, row_fn):
    return pl.BlockSpec((None, 1, D), lambda *ids: ((l * 8 + row_fn(*ids)) * 6 + k, 0, 0))


def _adaln_kernel(c_ref, w_ref, b_ref, o_ref):
    s = _silu(c_ref[...])
    w = w_ref[...]
    s_hi = s.astype(BF)
    s_lo = (s - s_hi.astype(F32)).astype(BF)
    w_hi = w.astype(BF)
    w_lo = (w - w_hi.astype(F32)).astype(BF)
    acc = (_dot(s_lo, w_hi) + _dot(s_hi, w_lo)) + _dot(s_hi, w_hi)
    o_ref[...] = acc + b_ref[...]


def _adaln(cond8, w_ada, b_ada):
    tn = 1536
    return pl.pallas_call(
        _adaln_kernel,
        grid=(DEPTH, 6 * D // tn),
        in_specs=[
            pl.BlockSpec((8, D), lambda l, j: (0, 0)),
            pl.BlockSpec((None, D, tn), lambda l, j: (l, 0, j)),
            pl.BlockSpec((None, 1, tn), lambda l, j: (l, 0, j)),
        ],
        out_specs=pl.BlockSpec((None, 8, tn), lambda l, j: (l, 0, j)),
        out_shape=jax.ShapeDtypeStruct((DEPTH, 8, 6 * D), F32),
        compiler_params=_params(("arbitrary", "arbitrary")),
        name="adaln",
    )(cond8, w_ada, b_ada.reshape(DEPTH, 1, 6 * D))


def _inproj_kernel(x_ref, g_ref, sh_ref, sc_ref, w_ref, wdt_ref, dtb_ref, o_ref, dt_ref, dtt_ref, h_ref):
    @pl.when(pl.program_id(1) == 0)
    def _():
        x = x_ref[...]
        y = x * lax.rsqrt(jnp.mean(x * x, axis=-1, keepdims=True) + EPS)
        h = (y * g_ref[...]) * (1.0 + sc_ref[...]) + sh_ref[...]
        hb = h.astype(BF)
        h_ref[...] = hb
        hl = (h - hb.astype(F32)).astype(BF)
        wd = wdt_ref[...]
        wh = wd.astype(BF)
        wl = (wd - wh.astype(F32)).astype(BF)
        d = (_dot(hl, wh) + _dot(hb, wl)) + _dot(hb, wh) + dtb_ref[...]
        dt = jnp.maximum(d, 0.0) + jnp.log1p(jnp.exp(-jnp.abs(d)))
        dt_ref[...] = dt
        dtt_ref[...] = dt.T

    o_ref[...] = _dot(h_ref[...], w_ref[...]).astype(BF)


def _in_proj(x2, g, mod3, l, row_fn, w_main, w_dt, dt_bias, tm):
    T = x2.shape[0]
    tn = 1024
    return pl.pallas_call(
        _inproj_kernel,
        grid=(T // tm, MAIN_W // tn),
        in_specs=[
            pl.BlockSpec((tm, D), lambda i, j: (i, 0)),
            pl.BlockSpec((1, D), lambda i, j: (0, 0)),
            _mod_spec(l, 0, lambda i, j: row_fn(i)),
            _mod_spec(l, 1, lambda i, j: row_fn(i)),
            pl.BlockSpec((D, tn), lambda i, j: (0, j)),
            pl.BlockSpec((D, DTW), lambda i, j: (0, 0)),
            pl.BlockSpec((1, DTW), lambda i, j: (0, 0)),
        ],
        out_specs=[
            pl.BlockSpec((tm, tn), lambda i, j: (i, j)),
            pl.BlockSpec((tm, DTW), lambda i, j: (i, 0)),
            pl.BlockSpec((DTW, tm), lambda i, j: (0, i)),
        ],
        out_shape=[
            jax.ShapeDtypeStruct((T, MAIN_W), BF),
            jax.ShapeDtypeStruct((T, DTW), F32),
            jax.ShapeDtypeStruct((DTW, T), F32),
        ],
        scratch_shapes=[pltpu.VMEM((tm, D), BF)],
        compiler_params=_params(("arbitrary", "arbitrary")),
        name="in_proj",
    )(x2, g, mod3, mod3, w_main, w_dt, dt_bias)


HALO = 16


def _conv_kernel(prev_ref, cur_ref, next_ref, w_ref, b_ref, o_ref, ext_ref, *, tl, nl, transpose):
    i = pl.program_id(1)
    ext_ref[0:HALO, :] = jnp.where(i > 0, prev_ref[...].astype(F32), 0.0)
    ext_ref[HALO:HALO + tl, :] = cur_ref[...].astype(F32)
    ext_ref[HALO + tl:2 * HALO + tl, :] = jnp.where(i < nl - 1, next_ref[...].astype(F32), 0.0)
    acc = b_ref[...]
    for k in range(KC):
        acc = acc + w_ref[k:k + 1, :] * ext_ref[pl.ds(HALO - KC // 2 + k, tl), :]
    out = _silu(acc)
    if transpose:
        out = out.T
    o_ref[...] = out.astype(BF)


def _conv(main, conv_w, conv_b, nb, L, transpose):
    T = nb * L
    tc = 512
    tl = min(L, 512)
    nl = L // tl
    rb = tl // HALO
    last = T // HALO - 1
    if transpose:
        ncol = GN // tc
        in_cb = lambda j: j + C_B // tc
        w_cb = lambda j: j + (C_B - C_X) // tc
        out_spec = pl.BlockSpec((tc, tl), lambda b, i, j: (j, b * nl + i))
        out_shape = jax.ShapeDtypeStruct((GN, T), BF)
    else:
        nx = DI // tc
        ncol = (DI + GN) // tc
        in_cb = lambda j: jnp.where(j < nx, j + C_X // tc, j - nx + C_C // tc)
        w_cb = lambda j: jnp.where(j < nx, j, j - nx + (C_C - C_X) // tc)
        out_spec = pl.BlockSpec((tl, tc), lambda b, i, j: (b * nl + i, j))
        out_shape = jax.ShapeDtypeStruct((T, DI + GN), BF)
    return pl.pallas_call(
        functools.partial(_conv_kernel, tl=tl, nl=nl, transpose=transpose),
        grid=(nb, nl, ncol),
        in_specs=[
            pl.BlockSpec((HALO, tc), lambda b, i, j: (jnp.maximum((b * nl + i) * rb - 1, 0), in_cb(j))),
            pl.BlockSpec((tl, tc), lambda b, i, j: (b * nl + i, in_cb(j))),
            pl.BlockSpec((HALO, tc), lambda b, i, j: (jnp.minimum((b * nl + i + 1) * rb, last), in_cb(j))),
            pl.BlockSpec((KC, tc), lambda b, i, j: (0, w_cb(j))),
            pl.BlockSpec((1, tc), lambda b, i, j: (0, w_cb(j))),
        ],
        out_specs=out_spec,
        out_shape=out_shape,
        scratch_shapes=[pltpu.VMEM((tl + 2 * HALO, tc), F32)],
        compiler_params=_params(("arbitrary", "arbitrary", "arbitrary")),
        name="conv_t" if transpose else "conv",
    )(main, main, main, conv_w, conv_b)


def _tri(kind):
    r = lax.broadcasted_iota(jnp.int32, (Q, Q), 0)
    c = lax.broadcasted_iota(jnp.int32, (Q, Q), 1)
    return (c <= r) if kind == "le" else (c >= r)


def _fill_blockdiag(bd_ref, xg):
    blk = lax.broadcasted_iota(jnp.int32, (Q, GW), 1) // P
    for r in range(R):
        bd_ref[r * Q:(r + 1) * Q, :] = jnp.where(blk == r, xg, jnp.zeros_like(xg))


def _ssd_state_kernel(xf_ref, btf_ref, dtf_ref, dttf_ref, xb_ref, btb_ref, dtb_ref, dttb_ref,
                      init_ref, alr_ref, alc_ref, e_ref,
                      entf_ref, entb_ref, fin_ref, st_ref, bd_ref, *, nc):
    c = pl.program_id(1)

    @pl.when(c == 0)
    def _():
        st_ref[...] = init_ref[...]

    a_row = -jnp.exp(alr_ref[...])
    a_col = -jnp.exp(alc_ref[...])
    dirs = ((xf_ref, btf_ref, dtf_ref, dttf_ref, entf_ref), (xb_ref, btb_ref, dtb_ref, dttb_ref, entb_ref))
    for d, (x_ref, bt_ref, dt_ref, dtt_ref, ent_ref) in enumerate(dirs):
        tri_t = _tri("ge" if d == 0 else "le").astype(BF)
        dtt = dtt_ref[...]
        acum_t = _dot3_lhs(dtt * a_col, tri_t)
        tot_c = acum_t[:, Q - 1:Q] if d == 0 else acum_t[:, 0:1]
        w_t = dtt * jnp.exp(tot_c - acum_t)
        tot_r = jnp.sum(dt_ref[...] * a_row, axis=0, keepdims=True)
        dec = jnp.broadcast_to(jnp.exp(tot_r), (8, DTW))
        dec_x = _dot3_lhs(dec, e_ref[:, d * DI:(d + 1) * DI])[0:1]
        for g in range(G):
            _fill_blockdiag(bd_ref, x_ref[:, g * GW:(g + 1) * GW])
            btg = bt_ref[g * N:(g + 1) * N, :].astype(F32)
            bw = [(btg * w_t[d * H + g * R + r:d * H + g * R + r + 1, :]).astype(BF) for r in range(R)]
            upd = _dot(jnp.concatenate(bw, axis=1), bd_ref[...])
            s = st_ref[d, g]
            ent_ref[g] = s.astype(BF)
            st_ref[d, g] = s * dec_x[:, g * GW:(g + 1) * GW] + upd

    @pl.when(c == nc - 1)
    def _():
        fin_ref[...] = st_ref[...]


def _ssd_state(xc, bt, dt, dtt, init, alog_r, alog_c, expand, nb, L):
    nc = L // Q
    fw = lambda b, c: b * nc + c
    bw = lambda b, c: b * nc + (nc - 1 - c)
    return pl.pallas_call(
        functools.partial(_ssd_state_kernel, nc=nc),
        grid=(nb, nc),
        in_specs=[
            pl.BlockSpec((Q, DI), lambda b, c: (fw(b, c), 0)),
            pl.BlockSpec((GN, Q), lambda b, c: (0, fw(b, c))),
            pl.BlockSpec((Q, DTW), lambda b, c: (fw(b, c), 0)),
            pl.BlockSpec((DTW, Q), lambda b, c: (0, fw(b, c))),
            pl.BlockSpec((Q, DI), lambda b, c: (bw(b, c), 0)),
            pl.BlockSpec((GN, Q), lambda b, c: (0, bw(b, c))),
            pl.BlockSpec((Q, DTW), lambda b, c: (bw(b, c), 0)),
            pl.BlockSpec((DTW, Q), lambda b, c: (0, bw(b, c))),
            pl.BlockSpec((None, 2, G, N, GW), lambda b, c: (b, 0, 0, 0, 0)),
            pl.BlockSpec((1, DTW), lambda b, c: (0, 0)),
            pl.BlockSpec((DTW, 1), lambda b, c: (0, 0)),
            pl.BlockSpec((DTW, 2 * DI), lambda b, c: (0, 0)),
        ],
        out_specs=[
            pl.BlockSpec((None, None, G, N, GW), lambda b, c: (b, c, 0, 0, 0)),
            pl.BlockSpec((None, None, G, N, GW), lambda b, c: (b, nc - 1 - c, 0, 0, 0)),
            pl.BlockSpec((None, 2, G, N, GW), lambda b, c: (b, 0, 0, 0, 0)),
        ],
        out_shape=[
            jax.ShapeDtypeStruct((nb, nc, G, N, GW), BF),
            jax.ShapeDtypeStruct((nb, nc, G, N, GW), BF),
            jax.ShapeDtypeStruct((nb, 2, G, N, GW), F32),
        ],
        scratch_shapes=[pltpu.VMEM((2, G, N, GW), F32), pltpu.VMEM((R * Q, GW), BF)],
        compiler_params=_params(("arbitrary", "arbitrary")),
        name="ssd_state",
    )(xc, bt, dt, dtt, xc, bt, dt, dtt, init, alog_r, alog_c, expand)


def _ssd_out_kernel(x_ref, c_ref, bt_ref, dt_ref, dtt_ref, z_ref, entf_ref, entb_ref,
                    alr_ref, alc_ref, dsk_ref, nw_ref, e2_ref, o_ref, bd_ref):
    a_row = -jnp.exp(alr_ref[...])
    a_col = -jnp.exp(alc_ref[...])
    dt = dt_ref[...]
    dtt = dtt_ref[...]
    a_n = dt * a_row
    a_t = dtt * a_col
    neg_inf = jnp.float32(-jnp.inf)
    acum, acum_t, e_x, mask = [], [], [], []
    for d in range(2):
        m = _tri("le" if d == 0 else "ge")
        mask.append(m)
        acum.append(_dot3_rhs(m.astype(BF), a_n))
        acum_t.append(_dot3_lhs(a_t, _tri("ge" if d == 0 else "le").astype(BF)))
        e = jnp.exp(acum[d])
        e_hi = e.astype(BF)
        e_lo = (e - e_hi.astype(F32)).astype(BF)
        e_x.append(_dot(jnp.concatenate([e_hi, e_lo], axis=1), e2_ref[:, d * DI:(d + 1) * DI]))
    ents = (entf_ref, entb_ref)
    dsum = dsk_ref[0:1, :] + dsk_ref[1:2, :]
    for g in range(G):
        cols = slice(g * GW, (g + 1) * GW)
        xg = x_ref[:, cols]
        cg = c_ref[:, g * N:(g + 1) * N]
        cb = _dot(cg, bt_ref[g * N:(g + 1) * N, :])
        _fill_blockdiag(bd_ref, xg)
        y = dsum[:, cols] * xg.astype(F32)
        for d in range(2):
            ms = []
            for r in range(R):
                k = d * H + g * R + r
                diff = acum[d][:, k:k + 1] - acum_t[d][k:k + 1, :]
                lmat = jnp.exp(jnp.where(mask[d], diff, neg_inf))
                ms.append(((cb * lmat) * dtt[k:k + 1, :]).astype(BF))
            y = y + _dot(jnp.concatenate(ms, axis=1), bd_ref[...])
            y = y + _dot(cg, ents[d][g]) * e_x[d][:, cols]
        zg = z_ref[:, cols].astype(F32)
        yz = y * _silu(zg)
        yn = yz * lax.rsqrt(jnp.mean(yz * yz, axis=-1, keepdims=True) + EPS)
        o_ref[:, cols] = (yn * nw_ref[:, cols]).astype(BF)


def _ssd_out(xc, bt, dt, dtt, main, entf, entb, alog_r, alog_c, dskip_x, norm_w, expand2, nb, L):
    nc = L // Q
    T = nb * L
    ch = lambda b, c: b * nc + c
    const = lambda b, c: (0, 0)
    return pl.pallas_call(
        _ssd_out_kernel,
        grid=(nb, nc),
        in_specs=[
            pl.BlockSpec((Q, DI), lambda b, c: (ch(b, c), 0)),
            pl.BlockSpec((Q, GN), lambda b, c: (ch(b, c), DI // GN)),
            pl.BlockSpec((GN, Q), lambda b, c: (0, ch(b, c))),
            pl.BlockSpec((Q, DTW), lambda b, c: (ch(b, c), 0)),
            pl.BlockSpec((DTW, Q), lambda b, c: (0, ch(b, c))),
            pl.BlockSpec((Q, DI), lambda b, c: (ch(b, c), C_Z // DI)),
            pl.BlockSpec((None, None, G, N, GW), lambda b, c: (b, c, 0, 0, 0)),
            pl.BlockSpec((None, None, G, N, GW), lambda b, c: (b, c, 0, 0, 0)),
            pl.BlockSpec((1, DTW), const),
            pl.BlockSpec((DTW, 1), const),
            pl.BlockSpec((2, DI), const),
            pl.BlockSpec((1, DI), const),
            pl.BlockSpec((2 * DTW, 2 * DI), const),
        ],
        out_specs=pl.BlockSpec((Q, DI), lambda b, c: (ch(b, c), 0)),
        out_shape=jax.ShapeDtypeStruct((T, DI), BF),
        scratch_shapes=[pltpu.VMEM((R * Q, GW), BF)],
        compiler_params=_params(("parallel", "parallel")),
        name="ssd_out",
    )(xc, xc, bt, dt, dtt, main, entf, entb, alog_r, alog_c, dskip_x, norm_w, expand2)


TM_MIX = 256


def _mixer_kernel(yn_ref, u_ref, gs_ref, gp_ref, x_ref, ga_ref, band_ref, cnt_ref, wso_ref, pw_ref,
                  psc_ref, wpo_ref, wo_ref, o_ref):
    o_ssd = _dot(yn_ref[...], wso_ref[...])
    pms = []
    for gi in range(len(POOL_WINDOWS)):
        ug = u_ref[:, gi * PG:(gi + 1) * PG]
        s = _dot(band_ref[gi], ug)
        pm = s / cnt_ref[:, gi:gi + 1] - ug.astype(F32)
        pms.append(_dot(pm.astype(BF), pw_ref[gi]))
    pmc = (jnp.concatenate(pms, axis=1) * psc_ref[...]).astype(BF)
    o_pool = _dot(pmc, wpo_ref[...])
    mix = jax.nn.sigmoid(gs_ref[...].astype(F32)) * o_ssd + jax.nn.sigmoid(gp_ref[...].astype(F32)) * o_pool
    o_ref[...] = x_ref[...] + ga_ref[...] * _dot(mix.astype(BF), wo_ref[...])


def _pool_tables(seg):
    t = np.arange(TM_MIX)
    same = (t[:, None] // seg) == (t[None, :] // seg)
    ts = t % seg
    bands, cnts = [], []
    for k in POOL_WINDOWS:
        lo = t[:, None] - k // 2
        hi = t[:, None] + k // 2
        bands.append(same & (t[None, :] >= lo) & (t[None, :] < hi))
        cnts.append(np.minimum(ts + k // 2, seg) - np.maximum(ts - k // 2, 0))
    cnt = np.ones((TM_MIX, 128), np.float32)
    cnt[:, :len(POOL_WINDOWS)] = np.stack(cnts, axis=1)
    return jnp.asarray(np.stack(bands).astype(np.float32), BF), jnp.asarray(cnt)


def _mixer_out(yn, main, x2, mod3, l, row_fn, seg, w_ssd_out, pool_w, pool_scale, w_pool_out, w_out):
    T = x2.shape[0]
    tm = TM_MIX
    band, cnt = _pool_tables(seg)
    c2 = lambda i: (0, 0)
    c3 = lambda i: (0, 0, 0)
    return pl.pallas_call(
        _mixer_kernel,
        grid=(T // tm,),
        in_specs=[
            pl.BlockSpec((tm, DI), lambda i: (i, 0)),
            pl.BlockSpec((tm, D), lambda i: (i, C_POOL // D)),
            pl.BlockSpec((tm, D), lambda i: (i, C_GS // D)),
            pl.BlockSpec((tm, D), lambda i: (i, C_GP // D)),
            pl.BlockSpec((tm, D), lambda i: (i, 0)),
            _mod_spec(l, 2, row_fn),
            pl.BlockSpec((len(POOL_WINDOWS), tm, tm), c3),
            pl.BlockSpec((tm, 128), c2),
            pl.BlockSpec((DI, D), c2),
            pl.BlockSpec((len(POOL_WINDOWS), PG, PG), c3),
            pl.BlockSpec((1, D), c2),
            pl.BlockSpec((D, D), c2),
            pl.BlockSpec((D, D), c2),
        ],
        out_specs=pl.BlockSpec((tm, D), lambda i: (i, 0)),
        out_shape=jax.ShapeDtypeStruct((T, D), F32),
        compiler_params=_params(("parallel",)),
        name="mixer_out",
    )(yn, main, main, main, x2, mod3, band, cnt, w_ssd_out, pool_w, pool_scale, w_pool_out, w_out)


TH = 1408


def _ffn_kernel(x_ref, g_ref, sh_ref, sc_ref, ga_ref, wg_ref, wu_ref, wd_ref, gf_ref, o_ref, h_ref, acc_ref,
                *, final):
    j = pl.program_id(1)

    @pl.when(j == 0)
    def _():
        x = x_ref[...]
        y = x * lax.rsqrt(jnp.mean(x * x, axis=-1, keepdims=True) + EPS)
        h_ref[...] = ((y * g_ref[...]) * (1.0 + sc_ref[...]) + sh_ref[...]).astype(BF)
        acc_ref[...] = jnp.zeros_like(acc_ref)

    h = h_ref[...]
    act = (_silu(_dot(h, wg_ref[...])) * _dot(h, wu_ref[...])).astype(BF)
    acc_ref[...] += _dot(act, wd_ref[...])

    @pl.when(j == pl.num_programs(1) - 1)
    def _():
        xn = x_ref[...] + ga_ref[...] * acc_ref[...]
        if final:
            xn = (xn * lax.rsqrt(jnp.mean(xn * xn, axis=-1, keepdims=True) + EPS)) * gf_ref[...]
        o_ref[...] = xn


def _ffn(x2, g, mod3, l, row_fn, w_gate_up, w_down, g_final, tm, final):
    T = x2.shape[0]
    nh = FH // TH
    rf = lambda i, j: row_fn(i)
    return pl.pallas_call(
        functools.partial(_ffn_kernel, final=final),
        grid=(T // tm, nh),
        in_specs=[
            pl.BlockSpec((tm, D), lambda i, j: (i, 0)),
            pl.BlockSpec((1, D), lambda i, j: (0, 0)),
            _mod_spec(l, 3, rf),
            _mod_spec(l, 4, rf),
            _mod_spec(l, 5, rf),
            pl.BlockSpec((D, TH), lambda i, j: (0, j)),
            pl.BlockSpec((D, TH), lambda i, j: (0, nh + j)),
            pl.BlockSpec((TH, D), lambda i, j: (j, 0)),
            pl.BlockSpec((1, D), lambda i, j: (0, 0)),
        ],
        out_specs=pl.BlockSpec((tm, D), lambda i, j: (i, 0)),
        out_shape=jax.ShapeDtypeStruct((T, D), F32),
        scratch_shapes=[pltpu.VMEM((tm, D), BF), pltpu.VMEM((tm, D), F32)],
        compiler_params=_params(("arbitrary", "arbitrary")),
        name="ffn",
    )(x2, g, mod3, mod3, mod3, w_gate_up, w_gate_up, w_down, g_final)


def _expand_table():
    e = np.zeros((DTW, 2 * DI), np.float32)
    for k in range(2 * H):
        e[k, k * P:(k + 1) * P] = 1.0
    return e


def kernel(x, c, ctx, c_ctx, w_ada, b_ada, g_mix, w_in, conv_w, conv_b, dt_bias, a_log, d_skip, ssd_norm_w,
           w_ssd_out, pool_w, pool_scale, w_pool_out, w_out, g_ffn, w_gate_up, w_down, g_final):
    nb, L, _ = x.shape
    Lc = ctx.shape[1]
    ctx_row = nb

    cond8 = jnp.zeros((8, D), F32).at[:nb].set(c).at[ctx_row].set(c_ctx)
    mod3 = _adaln(cond8, w_ada, b_ada).reshape(DEPTH * 8 * 6, 1, D)

    e_np = _expand_table()
    expand = jnp.asarray(e_np, BF)
    expand2 = jnp.asarray(np.concatenate([e_np, e_np], axis=0), BF)

    xl = x.reshape(nb * L, D)
    xc_ = ctx.reshape(nb * Lc, D)
    out = None
    for l in range(DEPTH):
        last = l == DEPTH - 1
        w_main = jnp.concatenate([w_in[l, :, :C_POOL], w_in[l, :, C_POOL + 2 * H:]], axis=1).astype(BF)
        w_dt = jnp.pad(w_in[l, :, C_POOL:C_POOL + 2 * H], ((0, 0), (0, DTW - 2 * H)))
        dtb = jnp.pad(dt_bias[l].reshape(1, 2 * H), ((0, 0), (0, DTW - 2 * H)))
        alog = jnp.pad(a_log[l].reshape(1, 2 * H), ((0, 0), (0, DTW - 2 * H)))
        alog_r, alog_c = alog, alog.reshape(DTW, 1)
        dskip_x = jnp.repeat(d_skip[l], P, axis=1)
        norm_w = ssd_norm_w[l].reshape(1, DI)
        gm = g_mix[l].reshape(1, D)
        gf = g_ffn[l].reshape(1, D)
        cw, cbias = conv_w[l], conv_b[l].reshape(1, -1)
        wso = w_ssd_out[l].astype(BF)
        pw = pool_w[l].astype(BF)
        psc = pool_scale[l].reshape(1, D)
        wpo = w_pool_out[l].astype(BF)
        wo = w_out[l].astype(BF)
        wgu = w_gate_up[l].astype(BF)
        wdn = w_down[l].astype(BF)
        gfin = g_final.reshape(1, D)

        def prepare(x2, row_fn, tm, seq):
            main, dt, dtt = _in_proj(x2, gm, mod3, l, row_fn, w_main, w_dt, dtb, tm)
            xcv = _conv(main, cw, cbias, nb, seq, transpose=False)
            bt = _conv(main, cw, cbias, nb, seq, transpose=True)
            return main, dt, dtt, xcv, bt

        def mix_and_ffn(x2, parts, ent, row_fn_mix, row_fn_ffn, seq, seg, tm_ffn, final):
            main, dt, dtt, xcv, bt = parts
            yn = _ssd_out(xcv, bt, dt, dtt, main, ent[0], ent[1], alog_r, alog_c, dskip_x, norm_w, expand2,
                          nb, seq)
            x2 = _mixer_out(yn, main, x2, mod3, l, row_fn_mix, seg, wso, pw, psc, wpo, wo)
            return _ffn(x2, gf, mod3, l, row_fn_ffn, wgu, wdn, gfin, tm_ffn, final)

        cparts = prepare(xc_, lambda i: ctx_row, 1024, Lc)
        zero = jnp.zeros((nb, 2, G, N, GW), F32)
        centf, centb, cfin = _ssd_state(cparts[3], cparts[4], cparts[1], cparts[2], zero, alog_r, alog_c,
                                        expand, nb, Lc)
        tm_l = 1024
        lparts = prepare(xl, lambda i: i // (L // tm_l), tm_l, L)
        lentf, lentb, _ = _ssd_state(lparts[3], lparts[4], lparts[1], lparts[2], cfin, alog_r, alog_c,
                                     expand, nb, L)
        tm_f = 512
        xl = mix_and_ffn(xl, lparts, (lentf, lentb), lambda i: i // (L // TM_MIX),
                         lambda i: i // (L // tm_f), L, GRID_W, tm_f, last)
        if not last:
            xc_ = mix_and_ffn(xc_, cparts, (centf, centb), lambda i: ctx_row, lambda i: ctx_row, Lc, Lc,
                              tm_f, False)
    return xl.reshape(nb, L, D)
```

```python
import functools

import numpy as np
import jax
import jax.numpy as jnp
from jax import lax
from jax.experimental import pallas as pl
from jax.experimental.pallas import tpu as pltpu

F32 = jnp.float32
BF = jnp.bfloat16

D = 1024
DEPTH = 2
H = 32
P = 64
G = 8
R = H // G
N = 128
Q = 128
DI = H * P
GN = G * N
GW = R * P
KC = 5
FH = 2816
POOL_WINDOWS = (2, 4, 8, 16)
NPW = len(POOL_WINDOWS)
PG = D // NPW
GRID_W = 64
EPS = 1e-6
LOG2E = 1.4426950408889634
DTW = 128

C_Z, C_X, C_B, C_C, C_POOL, C_GS, C_GP = 0, 2048, 4096, 5120, 6144, 7168, 8192
MAIN_W = 9216

VMEM_LIMIT = 56 * 1024 * 1024


def _dot(a, b):
    return jnp.dot(a, b, preferred_element_type=F32)


def _dot_nt(a, b):
    return lax.dot_general(a, b, (((1,), (1,)), ((), ())), preferred_element_type=F32)


def _split2(a):
    hi = a.astype(BF)
    return hi, (a - hi.astype(F32)).astype(BF)


def _split3(a):
    hi = a.astype(BF)
    r = a - hi.astype(F32)
    mid = r.astype(BF)
    lo = (r - mid.astype(F32)).astype(BF)
    return hi, mid, lo


def _dot3_rhs(m, a):
    hi, mid, lo = _split3(a)
    return (_dot(m, lo) + _dot(m, mid)) + _dot(m, hi)


def _dot3_lhs(a, m):
    hi, mid, lo = _split3(a)
    return (_dot(lo, m) + _dot(mid, m)) + _dot(hi, m)


def _silu(v):
    return v * jax.nn.sigmoid(v)


def _params(sem):
    return pltpu.CompilerParams(dimension_semantics=sem, vmem_limit_bytes=VMEM_LIMIT)


def _mod_spec(l, k, row_fn):
    return pl.BlockSpec((None, 1, D), lambda *ids: ((l * 8 + row_fn(*ids)) * 6 + k, 0, 0))


def _layer_spec(shape, l):
    zeros = (0,) * len(shape)
    return pl.BlockSpec((None,) + tuple(shape), lambda *ids: (l,) + zeros)


def _adaln_kernel(c_ref, w_ref, b_ref, o_ref):
    s_hi, s_lo = _split2(_silu(c_ref[...]))
    w_hi, w_lo = _split2(w_ref[...])
    acc = (_dot(s_lo, w_hi) + _dot(s_hi, w_lo)) + _dot(s_hi, w_hi)
    o_ref[...] = acc + b_ref[...]


def _adaln(cond8, w_ada, b_ada):
    tn = 1536
    return pl.pallas_call(
        _adaln_kernel,
        grid=(DEPTH, 6 * D // tn),
        in_specs=[
            pl.BlockSpec((8, D), lambda l, j: (0, 0)),
            pl.BlockSpec((None, D, tn), lambda l, j: (l, 0, j)),
            pl.BlockSpec((None, 1, tn), lambda l, j: (l, 0, j)),
        ],
        out_specs=pl.BlockSpec((None, 8, tn), lambda l, j: (l, 0, j)),
        out_shape=jax.ShapeDtypeStruct((DEPTH, 8, 6 * D), F32),
        compiler_params=_params(("arbitrary", "arbitrary")),
        name="adaln",
    )(cond8, w_ada, b_ada.reshape(DEPTH, 1, 6 * D))


def _inproj_kernel(x_ref, g_ref, sh_ref, sc_ref, w_ref, wdt_ref, dtb_ref, o_ref, dt_ref, dtt_ref, h_ref):
    @pl.when(pl.program_id(1) == 0)
    def _():
        x = x_ref[...]
        y = x * lax.rsqrt(jnp.mean(x * x, axis=-1, keepdims=True) + EPS)
        h = (y * g_ref[...]) * (1.0 + sc_ref[...]) + sh_ref[...]
        hb, hl = _split2(h)
        h_ref[...] = hb
        wh, wl = _split2(wdt_ref[...])
        d = (_dot_nt(hl, wh) + _dot_nt(hb, wl)) + _dot_nt(hb, wh) + dtb_ref[...]
        dt = jnp.maximum(d, 0.0) + jnp.log1p(jnp.exp(-jnp.abs(d)))
        dt_ref[...] = dt
        dtt_ref[...] = dt.T

    o_ref[...] = _dot_nt(h_ref[...], w_ref[...]).astype(BF)


def _in_proj(x2, g, mod3, l, row_fn, w_main, w_dt, dt_bias, tm):
    T = x2.shape[0]
    tn = 1024
    return pl.pallas_call(
        _inproj_kernel,
        grid=(T // tm, MAIN_W // tn),
        in_specs=[
            pl.BlockSpec((tm, D), lambda i, j: (i, 0)),
            _layer_spec((1, D), l),
            _mod_spec(l, 0, lambda i, j: row_fn(i)),
            _mod_spec(l, 1, lambda i, j: row_fn(i)),
            pl.BlockSpec((None, tn, D), lambda i, j: (l, j, 0)),
            _layer_spec((DTW, D), l),
            _layer_spec((1, DTW), l),
        ],
        out_specs=[
            pl.BlockSpec((tm, tn), lambda i, j: (i, j)),
            pl.BlockSpec((tm, DTW), lambda i, j: (i, 0)),
            pl.BlockSpec((DTW, tm), lambda i, j: (0, i)),
        ],
        out_shape=[
            jax.ShapeDtypeStruct((T, MAIN_W), BF),
            jax.ShapeDtypeStruct((T, DTW), F32),
            jax.ShapeDtypeStruct((DTW, T), F32),
        ],
        scratch_shapes=[pltpu.VMEM((tm, D), BF)],
        compiler_params=_params(("arbitrary", "arbitrary")),
        name="in_proj",
    )(x2, g, mod3, mod3, w_main, w_dt, dt_bias)


CH = 64


def _shift_table():
    s = np.zeros((KC - 1, Q, 2 * Q), np.float32)
    t = np.arange(Q)
    for si, k in enumerate([k for k in range(KC) if k != KC // 2]):
        s[si, t, CH + t + k - KC // 2] = 1.0
    return s.reshape((KC - 1) * Q, 2 * Q)


def _conv_kernel(prev_ref, cur_ref, next_ref, s_ref, w_ref, b_ref, o_ref, ext_ref, *, tl, nl, transpose):
    i = pl.program_id(1)
    prev = prev_ref[...]
    nxt = next_ref[...]
    ext_ref[0:CH, :] = jnp.where(i > 0, prev, jnp.zeros_like(prev))
    ext_ref[CH:CH + tl, :] = cur_ref[...]
    ext_ref[CH + tl:2 * CH + tl, :] = jnp.where(i < nl - 1, nxt, jnp.zeros_like(nxt))
    for r in range(tl // Q):
        win = ext_ref[r * Q:(r + 2) * Q, :]
        sh = _dot(s_ref[...], win)
        acc = b_ref[...]
        si = 0
        for k in range(KC):
            if k == KC // 2:
                tap = win[CH:CH + Q, :].astype(F32)
            else:
                tap = sh[si * Q:(si + 1) * Q, :]
                si += 1
            acc = acc + w_ref[k:k + 1, :] * tap
        out = _silu(acc)
        if transpose:
            o_ref[:, r * Q:(r + 1) * Q] = out.T.astype(BF)
        else:
            o_ref[r * Q:(r + 1) * Q, :] = out.astype(BF)


def _conv(main, shift, conv_w, conv_b, l, nb, L, transpose):
    T = nb * L
    tc = 512
    tl = min(L, 512)
    nl = L // tl
    rb = tl // CH
    last = T // CH - 1
    if transpose:
        ncol = GN // tc
        in_cb = lambda j: j + C_B // tc
        w_cb = lambda j: j + (C_B - C_X) // tc
        out_spec = pl.BlockSpec((tc, tl), lambda b, i, j: (j, b * nl + i))
        out_shape = jax.ShapeDtypeStruct((GN, T), BF)
    else:
        nx = DI // tc
        ncol = (DI + GN) // tc
        in_cb = lambda j: jnp.where(j < nx, j + C_X // tc, j - nx + C_C // tc)
        w_cb = lambda j: jnp.where(j < nx, j, j - nx + (C_C - C_X) // tc)
        out_spec = pl.BlockSpec((tl, tc), lambda b, i, j: (b * nl + i, j))
        out_shape = jax.ShapeDtypeStruct((T, DI + GN), BF)
    return pl.pallas_call(
        functools.partial(_conv_kernel, tl=tl, nl=nl, transpose=transpose),
        grid=(nb, nl, ncol),
        in_specs=[
            pl.BlockSpec((CH, tc), lambda b, i, j: (jnp.maximum((b * nl + i) * rb - 1, 0), in_cb(j))),
            pl.BlockSpec((tl, tc), lambda b, i, j: (b * nl + i, in_cb(j))),
            pl.BlockSpec((CH, tc), lambda b, i, j: (jnp.minimum((b * nl + i + 1) * rb, last), in_cb(j))),
            pl.BlockSpec(((KC - 1) * Q, 2 * Q), lambda b, i, j: (0, 0)),
            pl.BlockSpec((None, KC, tc), lambda b, i, j: (l, 0, w_cb(j))),
            pl.BlockSpec((None, 1, tc), lambda b, i, j: (l, 0, w_cb(j))),
        ],
        out_specs=out_spec,
        out_shape=out_shape,
        scratch_shapes=[pltpu.VMEM((tl + 2 * CH, tc), BF)],
        compiler_params=_params(("arbitrary", "arbitrary", "arbitrary")),
        name="conv_t" if transpose else "conv",
    )(main, main, main, shift, conv_w, conv_b)


def _tri(kind):
    r = lax.broadcasted_iota(jnp.int32, (Q, Q), 0)
    c = lax.broadcasted_iota(jnp.int32, (Q, Q), 1)
    return (c <= r) if kind == "le" else (c >= r)


def _expand_table():
    e = np.zeros((DTW, 2 * DI), np.float32)
    for k in range(2 * H):
        e[k, k * P:(k + 1) * P] = 1.0
    return np.concatenate([e, e], axis=0)


XROWS = 16


def _expand_lhs(w, dec):
    w_hi, w_lo = _split2(w)
    d_hi, d_mid, d_lo = [t.astype(F32) for t in _split3(jnp.broadcast_to(dec, (XROWS, DTW)))]
    rid = lax.broadcasted_iota(jnp.int32, (XROWS, DTW), 0)
    extra = jnp.where(rid == 0, d_hi, jnp.where(rid == 1, d_mid, jnp.where(rid == 2, d_lo, 0.0))).astype(BF)
    zero = jnp.zeros((XROWS, DTW), BF)
    return jnp.concatenate([jnp.concatenate([w_hi, w_lo], axis=1), jnp.concatenate([extra, zero], axis=1)], axis=0)


def _ssd_state_kernel(*refs, nc, has_init):
    if has_init:
        (xf_ref, btf_ref, dtf_ref, xb_ref, btb_ref, dtb_ref, init_ref, alr_ref, e2_ref,
         entf_ref, entb_ref, fin_ref, st_ref, xw_ref, dec_ref) = refs
    else:
        (xf_ref, btf_ref, dtf_ref, xb_ref, btb_ref, dtb_ref, alr_ref, e2_ref,
         entf_ref, entb_ref, fin_ref, st_ref, xw_ref, dec_ref) = refs
    c = pl.program_id(1)

    @pl.when(c == 0)
    def _():
        st_ref[...] = init_ref[...] if has_init else jnp.zeros_like(st_ref)

    a_row = -jnp.exp(alr_ref[...])
    dirs = ((xf_ref, btf_ref, dtf_ref, entf_ref), (xb_ref, btb_ref, dtb_ref, entb_ref))
    for d, (x_ref, bt_ref, dt_ref, ent_ref) in enumerate(dirs):
        dt = dt_ref[...]
        acum = _dot3_rhs(_tri("le" if d == 0 else "ge").astype(BF), dt * a_row)
        tot = acum[Q - 1:Q] if d == 0 else acum[0:1]
        ex = _dot(_expand_lhs(dt * jnp.exp(tot - acum), jnp.exp(tot)), e2_ref[:, d * DI:(d + 1) * DI])
        dec_ref[d] = (ex[Q + 2:Q + 3] + ex[Q + 1:Q + 2]) + ex[Q:Q + 1]
        xw_ref[d] = (x_ref[...].astype(F32) * ex[0:Q]).astype(BF)
    for d, (x_ref, bt_ref, dt_ref, ent_ref) in enumerate(dirs):
        for g in range(G):
            cols = slice(g * GW, (g + 1) * GW)
            upd = _dot(bt_ref[g * N:(g + 1) * N, :], xw_ref[d, :, cols])
            s = st_ref[d, g]
            ent_ref[g] = s.astype(BF)
            st_ref[d, g] = s * dec_ref[d, :, cols] + upd

    @pl.when(c == nc - 1)
    def _():
        fin_ref[...] = st_ref[...]


def _ssd_state(xc, bt, dt, init, alog_r, expand2, l, nb, L):
    nc = L // Q
    fw = lambda b, c: b * nc + c
    bw = lambda b, c: b * nc + (nc - 1 - c)
    has_init = init is not None
    chunk_specs = lambda ch: [
        pl.BlockSpec((Q, DI), lambda b, c: (ch(b, c), 0)),
        pl.BlockSpec((GN, Q), lambda b, c: (0, ch(b, c))),
        pl.BlockSpec((Q, DTW), lambda b, c: (ch(b, c), 0)),
    ]
    st_spec = pl.BlockSpec((None, 2, G, N, GW), lambda b, c: (b, 0, 0, 0, 0))
    in_specs = chunk_specs(fw) + chunk_specs(bw) + ([st_spec] if has_init else []) + [
        _layer_spec((1, DTW), l),
        pl.BlockSpec((2 * DTW, 2 * DI), lambda b, c: (0, 0)),
    ]
    args = (xc, bt, dt, xc, bt, dt) + ((init,) if has_init else ()) + (alog_r, expand2)
    return pl.pallas_call(
        functools.partial(_ssd_state_kernel, nc=nc, has_init=has_init),
        grid=(nb, nc),
        in_specs=in_specs,
        out_specs=[
            pl.BlockSpec((None, None, G, N, GW), lambda b, c: (b, c, 0, 0, 0)),
            pl.BlockSpec((None, None, G, N, GW), lambda b, c: (b, nc - 1 - c, 0, 0, 0)),
            st_spec,
        ],
        out_shape=[
            jax.ShapeDtypeStruct((nb, nc, G, N, GW), BF),
            jax.ShapeDtypeStruct((nb, nc, G, N, GW), BF),
            jax.ShapeDtypeStruct((nb, 2, G, N, GW), F32),
        ],
        scratch_shapes=[pltpu.VMEM((2, G, N, GW), F32), pltpu.VMEM((2, Q, DI), BF), pltpu.VMEM((2, 1, DI), F32)],
        compiler_params=_params(("arbitrary", "arbitrary")),
        name="ssd_state",
    )(*args)


def _fill_blockdiag(bd_ref, xg):
    blk = lax.broadcasted_iota(jnp.int32, (Q, GW), 1) // P
    for r in range(R):
        bd_ref[r * Q:(r + 1) * Q, :] = jnp.where(blk == r, xg, jnp.zeros_like(xg))


def _ssd_out_kernel(x_ref, c_ref, bt_ref, dt_ref, dtt_ref, entf_ref, entb_ref,
                    alr_ref, alc_ref, dsk_ref, o_ref, bd_ref):
    a_row = -jnp.exp(alr_ref[...])
    a_col = -jnp.exp(alc_ref[...])
    dt = dt_ref[...]
    dtt = dtt_ref[...]
    log_dtt = jnp.log(dtt)
    neg_inf = jnp.float32(-jnp.inf)
    left = lax.broadcasted_iota(jnp.int32, (Q, 2 * P), 1) < P
    acum, rowv, mask = [], [], []
    for d in range(2):
        m = _tri("le" if d == 0 else "ge")
        mask.append(m)
        acum.append(_dot3_rhs(m.astype(BF), dt * a_row) * LOG2E)
        acum_t = _dot3_lhs(dtt * a_col, _tri("ge" if d == 0 else "le").astype(BF))
        rowv.append((log_dtt - acum_t) * LOG2E)
    ents = (entf_ref, entb_ref)
    dsum = dsk_ref[0:1, :] + dsk_ref[1:2, :]
    for g in range(G):
        cols = slice(g * GW, (g + 1) * GW)
        xg = x_ref[:, cols]
        cg = c_ref[:, g * N:(g + 1) * N]
        cb = _dot(cg, bt_ref[g * N:(g + 1) * N, :])
        _fill_blockdiag(bd_ref.at[g], xg)
        y = dsum[:, cols] * xg.astype(F32)
        for d in range(2):
            ms, e_x = [], []
            for r in range(0, R, 2):
                es = []
                for k in (d * H + g * R + r, d * H + g * R + r + 1):
                    acol = jnp.broadcast_to(acum[d][:, k:k + 1], (Q, Q))
                    expo = acol + rowv[d][k:k + 1, :]
                    ms.append((cb * jnp.exp2(jnp.where(mask[d], expo, neg_inf))).astype(BF))
                    es.append(jnp.exp2(acol))
                e_x.append(jnp.where(left, es[0], es[1]))
            y = y + _dot(jnp.concatenate(ms, axis=1), bd_ref[g])
            y = y + _dot(cg, ents[d][g]) * jnp.concatenate(e_x, axis=1)
        o_ref[:, cols] = y.astype(BF)


def _ssd_out(xc, bt, dt, dtt, entf, entb, alog_r, alog_c, dskip_x, l, nb, L):
    nc = L // Q
    T = nb * L
    ch = lambda b, c: b * nc + c
    ent_spec = pl.BlockSpec((None, None, G, N, GW), lambda b, c: (b, c, 0, 0, 0))
    return pl.pallas_call(
        _ssd_out_kernel,
        grid=(nb, nc),
        in_specs=[
            pl.BlockSpec((Q, DI), lambda b, c: (ch(b, c), 0)),
            pl.BlockSpec((Q, GN), lambda b, c: (ch(b, c), DI // GN)),
            pl.BlockSpec((GN, Q), lambda b, c: (0, ch(b, c))),
            pl.BlockSpec((Q, DTW), lambda b, c: (ch(b, c), 0)),
            pl.BlockSpec((DTW, Q), lambda b, c: (0, ch(b, c))),
            ent_spec,
            ent_spec,
            _layer_spec((1, DTW), l),
            _layer_spec((DTW, 1), l),
            _layer_spec((2, DI), l),
        ],
        out_specs=pl.BlockSpec((Q, DI), lambda b, c: (ch(b, c), 0)),
        out_shape=jax.ShapeDtypeStruct((T, DI), BF),
        scratch_shapes=[pltpu.VMEM((G, R * Q, GW), BF)],
        compiler_params=_params(("parallel", "parallel")),
        name="ssd_out",
    )(xc, xc, bt, dt, dtt, entf, entb, alog_r, alog_c, dskip_x)


TM_MIX = 256
MIX_SUB = 2


def _mixer_kernel(y_ref, z_ref, u_ref, gs_ref, gp_ref, x_ref, ga_ref, nw_ref, band_ref, cnt_ref, wso_ref,
                  pw_ref, psc_ref, wpo_ref, wo_ref, o_ref):
    for s in range(MIX_SUB):
        rows = slice(s * TM_MIX, (s + 1) * TM_MIX)
        yns = []
        for g in range(G):
            cols = slice(g * GW, (g + 1) * GW)
            yz = y_ref[rows, cols].astype(F32) * _silu(z_ref[rows, cols].astype(F32))
            yn = yz * lax.rsqrt(jnp.mean(yz * yz, axis=-1, keepdims=True) + EPS)
            yns.append((yn * nw_ref[:, cols]).astype(BF))
        o_ssd = _dot(jnp.concatenate(yns, axis=1), wso_ref[...])
        pms = []
        for gi in range(NPW):
            ug = u_ref[rows, gi * PG:(gi + 1) * PG]
            wsum = _dot(band_ref[gi], ug)
            pm = wsum / cnt_ref[:, gi:gi + 1] - ug.astype(F32)
            pms.append(_dot(pm.astype(BF), pw_ref[gi]))
        pmc = (jnp.concatenate(pms, axis=1) * psc_ref[...]).astype(BF)
        o_pool = _dot(pmc, wpo_ref[...])
        mix = (jax.nn.sigmoid(gs_ref[rows, :].astype(F32)) * o_ssd
               + jax.nn.sigmoid(gp_ref[rows, :].astype(F32)) * o_pool)
        o_ref[rows, :] = x_ref[rows, :] + ga_ref[...] * _dot(mix.astype(BF), wo_ref[...])


def _pool_tables(seg):
    t = np.arange(TM_MIX)
    same = (t[:, None] // seg) == (t[None, :] // seg)
    ts = t % seg
    bands, cnts = [], []
    for k in POOL_WINDOWS:
        lo = t[:, None] - k // 2
        hi = t[:, None] + k // 2
        bands.append(same & (t[None, :] >= lo) & (t[None, :] < hi))
        cnts.append(np.minimum(ts + k // 2, seg) - np.maximum(ts - k // 2, 0))
    cnt = np.ones((TM_MIX, 128), np.float32)
    cnt[:, :NPW] = np.stack(cnts, axis=1)
    return jnp.asarray(np.stack(bands).astype(np.float32), BF), jnp.asarray(cnt)


def _mixer_out(y, main, x2, mod3, l, row_fn, seg, norm_w, w_ssd_out, pool_w, pool_scale, w_pool_out, w_out):
    T = x2.shape[0]
    tm = TM_MIX * MIX_SUB
    band, cnt = _pool_tables(seg)
    return pl.pallas_call(
        _mixer_kernel,
        grid=(T // tm,),
        in_specs=[
            pl.BlockSpec((tm, DI), lambda i: (i, 0)),
            pl.BlockSpec((tm, DI), lambda i: (i, C_Z // DI)),
            pl.BlockSpec((tm, D), lambda i: (i, C_POOL // D)),
            pl.BlockSpec((tm, D), lambda i: (i, C_GS // D)),
            pl.BlockSpec((tm, D), lambda i: (i, C_GP // D)),
            pl.BlockSpec((tm, D), lambda i: (i, 0)),
            _mod_spec(l, 2, row_fn),
            _layer_spec((1, DI), l),
            pl.BlockSpec((NPW, TM_MIX, TM_MIX), lambda i: (0, 0, 0)),
            pl.BlockSpec((TM_MIX, 128), lambda i: (0, 0)),
            _layer_spec((DI, D), l),
            _layer_spec((NPW, PG, PG), l),
            _layer_spec((1, D), l),
            _layer_spec((D, D), l),
            _layer_spec((D, D), l),
        ],
        out_specs=pl.BlockSpec((tm, D), lambda i: (i, 0)),
        out_shape=jax.ShapeDtypeStruct((T, D), F32),
        compiler_params=_params(("parallel",)),
        name="mixer_out",
    )(y, main, main, main, main, x2, mod3, norm_w, band, cnt, w_ssd_out, pool_w, pool_scale, w_pool_out, w_out)


TH = 1408


def _ffn_kernel(x_ref, g_ref, sh_ref, sc_ref, ga_ref, wg_ref, wu_ref, wd_ref, gf_ref, o_ref, h_ref, acc_ref,
                *, final):
    j = pl.program_id(1)

    @pl.when(j == 0)
    def _():
        x = x_ref[...]
        y = x * lax.rsqrt(jnp.mean(x * x, axis=-1, keepdims=True) + EPS)
        h_ref[...] = ((y * g_ref[...]) * (1.0 + sc_ref[...]) + sh_ref[...]).astype(BF)
        acc_ref[...] = jnp.zeros_like(acc_ref)

    h = h_ref[...]
    act = (_silu(_dot(h, wg_ref[...])) * _dot(h, wu_ref[...])).astype(BF)
    acc_ref[...] += _dot(act, wd_ref[...])

    @pl.when(j == pl.num_programs(1) - 1)
    def _():
        xn = x_ref[...] + ga_ref[...] * acc_ref[...]
        if final:
            xn = (xn * lax.rsqrt(jnp.mean(xn * xn, axis=-1, keepdims=True) + EPS)) * gf_ref[...]
        o_ref[...] = xn


def _ffn(x2, g, mod3, l, row_fn, w_gate_up, w_down, g_final, tm, final):
    T = x2.shape[0]
    nh = FH // TH
    rf = lambda i, j: row_fn(i)
    return pl.pallas_call(
        functools.partial(_ffn_kernel, final=final),
        grid=(T // tm, nh),
        in_specs=[
            pl.BlockSpec((tm, D), lambda i, j: (i, 0)),
            _layer_spec((1, D), l),
            _mod_spec(l, 3, rf),
            _mod_spec(l, 4, rf),
            _mod_spec(l, 5, rf),
            pl.BlockSpec((None, D, TH), lambda i, j: (l, 0, j)),
            pl.BlockSpec((None, D, TH), lambda i, j: (l, 0, nh + j)),
            pl.BlockSpec((None, TH, D), lambda i, j: (l, j, 0)),
            pl.BlockSpec((1, D), lambda i, j: (0, 0)),
        ],
        out_specs=pl.BlockSpec((tm, D), lambda i, j: (i, 0)),
        out_shape=jax.ShapeDtypeStruct((T, D), F32),
        scratch_shapes=[pltpu.VMEM((tm, D), BF), pltpu.VMEM((tm, D), F32)],
        compiler_params=_params(("arbitrary", "arbitrary")),
        name="ffn",
    )(x2, g, mod3, mod3, mod3, w_gate_up, w_gate_up, w_down, g_final)


def kernel(x, c, ctx, c_ctx, w_ada, b_ada, g_mix, w_in, conv_w, conv_b, dt_bias, a_log, d_skip, ssd_norm_w,
           w_ssd_out, pool_w, pool_scale, w_pool_out, w_out, g_ffn, w_gate_up, w_down, g_final):
    nb, L, _ = x.shape
    Lc = ctx.shape[1]
    ctx_row = nb

    cond8 = jnp.zeros((8, D), F32).at[:nb].set(c).at[ctx_row].set(c_ctx)
    mod3 = _adaln(cond8, w_ada, b_ada).reshape(DEPTH * 8 * 6, 1, D)

    shift = jnp.asarray(_shift_table(), BF)
    expand2 = jnp.asarray(_expand_table(), BF)

    w_in_t = jnp.swapaxes(w_in, 1, 2)
    w_main = jnp.concatenate([w_in_t[:, :C_POOL], w_in_t[:, C_POOL + 2 * H:]], axis=1).astype(BF)
    w_dt = jnp.pad(w_in_t[:, C_POOL:C_POOL + 2 * H], ((0, 0), (0, DTW - 2 * H), (0, 0)))
    pad_heads = lambda a: jnp.pad(a.reshape(DEPTH, 1, 2 * H), ((0, 0), (0, 0), (0, DTW - 2 * H)))
    dtb = pad_heads(dt_bias)
    alog_r = pad_heads(a_log)
    alog_c = alog_r.reshape(DEPTH, DTW, 1)
    dskip_x = jnp.repeat(d_skip, P, axis=2)
    norm_w = ssd_norm_w.reshape(DEPTH, 1, DI)
    gm = g_mix.reshape(DEPTH, 1, D)
    gf = g_ffn.reshape(DEPTH, 1, D)
    cbias = conv_b.reshape(DEPTH, 1, -1)
    wso = w_ssd_out.astype(BF)
    pw = pool_w.astype(BF)
    psc = pool_scale.reshape(DEPTH, 1, D)
    wpo = w_pool_out.astype(BF)
    wo = w_out.astype(BF)
    wgu = w_gate_up.astype(BF)
    wdn = w_down.astype(BF)
    gfin = g_final.reshape(1, D)

    xl = x.reshape(nb * L, D)
    xc_ = ctx.reshape(nb * Lc, D)
    tm_l = 1024
    tm_f = 512
    for l in range(DEPTH):
        last = l == DEPTH - 1

        def prepare(x2, row_fn, tm, seq):
            main, dt, dtt = _in_proj(x2, gm, mod3, l, row_fn, w_main, w_dt, dtb, tm)
            xcv = _conv(main, shift, conv_w, cbias, l, nb, seq, transpose=False)
            bt = _conv(main, shift, conv_w, cbias, l, nb, seq, transpose=True)
            return main, dt, dtt, xcv, bt

        def mix_and_ffn(x2, parts, ent, row_fn_mix, row_fn_ffn, seq, seg, final):
            main, dt, dtt, xcv, bt = parts
            y = _ssd_out(xcv, bt, dt, dtt, ent[0], ent[1], alog_r, alog_c, dskip_x, l, nb, seq)
            x2 = _mixer_out(y, main, x2, mod3, l, row_fn_mix, seg, norm_w, wso, pw, psc, wpo, wo)
            return _ffn(x2, gf, mod3, l, row_fn_ffn, wgu, wdn, gfin, tm_f, final)

        cparts = prepare(xc_, lambda i: ctx_row, 1024, Lc)
        centf, centb, cfin = _ssd_state(cparts[3], cparts[4], cparts[1], None, alog_r, expand2, l, nb, Lc)
        lparts = prepare(xl, lambda i: i // (L // tm_l), tm_l, L)
        lentf, lentb, _ = _ssd_state(lparts[3], lparts[4], lparts[1], cfin, alog_r, expand2, l, nb, L)
        xl = mix_and_ffn(xl, lparts, (lentf, lentb), lambda i: i // (L // (TM_MIX * MIX_SUB)),
                         lambda i: i // (L // tm_f), L, GRID_W, last)
        if not last:
            xc_ = mix_and_ffn(xc_, cparts, (centf, centb), lambda i: ctx_row, lambda i: ctx_row, Lc, Lc, False)
    return xl.reshape(nb, L, D)
```

```python
import functools

import numpy as np
import jax
import jax.numpy as jnp
from jax import lax
from jax.experimental import pallas as pl
from jax.experimental.pallas import tpu as pltpu

F32 = jnp.float32
BF = jnp.bfloat16

D = 1024
DEPTH = 2
H = 32
P = 64
G = 8
R = H // G
N = 128
Q = 128
DI = H * P
GN = G * N
GW = R * P
KC = 5
FH = 2816
POOL_WINDOWS = (2, 4, 8, 16)
NPW = len(POOL_WINDOWS)
PG = D // NPW
GRID_W = 64
EPS = 1e-6
LOG2E = 1.4426950408889634
DTW = 128

C_Z, C_X, C_B, C_C, C_POOL, C_GS, C_GP = 0, 2048, 4096, 5120, 6144, 7168, 8192
MAIN_W = 9216

VMEM_LIMIT = 56 * 1024 * 1024


def _dot(a, b):
    return jnp.dot(a, b, preferred_element_type=F32)


def _dot_nt(a, b):
    return lax.dot_general(a, b, (((1,), (1,)), ((), ())), preferred_element_type=F32)


def _split2(a):
    hi = a.astype(BF)
    return hi, (a - hi.astype(F32)).astype(BF)


def _split3(a):
    hi = a.astype(BF)
    r = a - hi.astype(F32)
    mid = r.astype(BF)
    lo = (r - mid.astype(F32)).astype(BF)
    return hi, mid, lo


def _dot3_rhs(m, a):
    hi, mid, lo = _split3(a)
    return (_dot(m, lo) + _dot(m, mid)) + _dot(m, hi)


def _dot3_lhs(a, m):
    hi, mid, lo = _split3(a)
    return (_dot(lo, m) + _dot(mid, m)) + _dot(hi, m)


def _silu(v):
    return v * jax.nn.sigmoid(v)


def _params(sem):
    return pltpu.CompilerParams(dimension_semantics=sem, vmem_limit_bytes=VMEM_LIMIT)


def _mod_spec(l, k, row_fn):
    return pl.BlockSpec((None, 1, D), lambda *ids: ((l * 8 + row_fn(*ids)) * 6 + k, 0, 0))


def _layer_spec(shape, l):
    zeros = (0,) * len(shape)
    return pl.BlockSpec((None,) + tuple(shape), lambda *ids: (l,) + zeros)


def _adaln_kernel(c_ref, w_ref, b_ref, o_ref):
    s_hi, s_lo = _split2(_silu(c_ref[...]))
    w_hi, w_lo = _split2(w_ref[...])
    acc = (_dot(s_lo, w_hi) + _dot(s_hi, w_lo)) + _dot(s_hi, w_hi)
    o_ref[...] = acc + b_ref[...]


def _adaln(cond8, w_ada, b_ada):
    tn = 1536
    return pl.pallas_call(
        _adaln_kernel,
        grid=(DEPTH, 6 * D // tn),
        in_specs=[
            pl.BlockSpec((8, D), lambda l, j: (0, 0)),
            pl.BlockSpec((None, D, tn), lambda l, j: (l, 0, j)),
            pl.BlockSpec((None, 1, tn), lambda l, j: (l, 0, j)),
        ],
        out_specs=pl.BlockSpec((None, 8, tn), lambda l, j: (l, 0, j)),
        out_shape=jax.ShapeDtypeStruct((DEPTH, 8, 6 * D), F32),
        compiler_params=_params(("arbitrary", "arbitrary")),
        name="adaln",
    )(cond8, w_ada, b_ada.reshape(DEPTH, 1, 6 * D))


def _inproj_kernel(x_ref, g_ref, sh_ref, sc_ref, w_ref, wdt_ref, dtb_ref, o_ref, dt_ref, dtt_ref, h_ref):
    @pl.when(pl.program_id(1) == 0)
    def _():
        x = x_ref[...]
        y = x * lax.rsqrt(jnp.mean(x * x, axis=-1, keepdims=True) + EPS)
        h = (y * g_ref[...]) * (1.0 + sc_ref[...]) + sh_ref[...]
        hb, hl = _split2(h)
        h_ref[...] = hb
        wh, wl = _split2(wdt_ref[...])
        d = (_dot_nt(hl, wh) + _dot_nt(hb, wl)) + _dot_nt(hb, wh) + dtb_ref[...]
        dt = jnp.maximum(d, 0.0) + jnp.log1p(jnp.exp(-jnp.abs(d)))
        dt_ref[...] = dt
        dtt_ref[...] = dt.T

    o_ref[...] = _dot_nt(h_ref[...], w_ref[...]).astype(BF)


def _in_proj(x2, g, mod3, l, row_fn, w_main, w_dt, dt_bias, tm):
    T = x2.shape[0]
    tn = 2304
    return pl.pallas_call(
        _inproj_kernel,
        grid=(T // tm, MAIN_W // tn),
        in_specs=[
            pl.BlockSpec((tm, D), lambda i, j: (i, 0)),
            _layer_spec((1, D), l),
            _mod_spec(l, 0, lambda i, j: row_fn(i)),
            _mod_spec(l, 1, lambda i, j: row_fn(i)),
            pl.BlockSpec((None, tn, D), lambda i, j: (l, j, 0)),
            _layer_spec((DTW, D), l),
            _layer_spec((1, DTW), l),
        ],
        out_specs=[
            pl.BlockSpec((tm, tn), lambda i, j: (i, j)),
            pl.BlockSpec((tm, DTW), lambda i, j: (i, 0)),
            pl.BlockSpec((DTW, tm), lambda i, j: (0, i)),
        ],
        out_shape=[
            jax.ShapeDtypeStruct((T, MAIN_W), BF),
            jax.ShapeDtypeStruct((T, DTW), F32),
            jax.ShapeDtypeStruct((DTW, T), F32),
        ],
        scratch_shapes=[pltpu.VMEM((tm, D), BF)],
        compiler_params=_params(("arbitrary", "arbitrary")),
        name="in_proj",
    )(x2, g, mod3, mod3, w_main, w_dt, dt_bias)


CH = 64


SUBL = 8
CONV_SUB = 256


def _shift_table():
    s = np.zeros((Q // SUBL, KC - 1, SUBL, 2 * Q), np.float32)
    t = np.arange(Q)
    for si, k in enumerate([k for k in range(KC) if k != KC // 2]):
        s[t // SUBL, si, t % SUBL, CH + t + k - KC // 2] = 1.0
    return s.reshape((KC - 1) * Q, 2 * Q)


def _conv_kernel(prev_ref, cur_ref, next_ref, s_ref, w_ref, b_ref, o_ref, ext_ref, *, tl, nl, transpose):
    i = pl.program_id(1)
    prev = prev_ref[...]
    nxt = next_ref[...]
    ext_ref[0:CH, :] = jnp.where(i > 0, prev, jnp.zeros_like(prev))
    ext_ref[CH:CH + tl, :] = cur_ref[...]
    ext_ref[CH + tl:2 * CH + tl, :] = jnp.where(i < nl - 1, nxt, jnp.zeros_like(nxt))
    tc = cur_ref.shape[1]
    for r in range(tl // Q):
        for c0 in range(0, tc, CONV_SUB):
            cols = slice(c0, c0 + CONV_SUB)
            win = ext_ref[r * Q:(r + 2) * Q, cols]
            sh = _dot(s_ref[...], win)
            acc = b_ref[:, cols]
            si = 0
            for k in range(KC):
                if k == KC // 2:
                    tap = win[CH:CH + Q, :].astype(F32)
                else:
                    starts = [(tg * (KC - 1) + si) * SUBL for tg in range(Q // SUBL)]
                    tap = jnp.concatenate([sh[a:a + SUBL, :] for a in starts], axis=0)
                    si += 1
                acc = acc + w_ref[k:k + 1, cols] * tap
            out = _silu(acc)
            if transpose:
                o_ref[cols, r * Q:(r + 1) * Q] = out.T.astype(BF)
            else:
                o_ref[r * Q:(r + 1) * Q, cols] = out.astype(BF)


def _conv(main, shift, conv_w, conv_b, l, nb, L, transpose):
    T = nb * L
    tc = 1024
    tl = min(L, 512)
    nl = L // tl
    rb = tl // CH
    last = T // CH - 1
    if transpose:
        ncol = GN // tc
        in_cb = lambda j: j + C_B // tc
        w_cb = lambda j: j + (C_B - C_X) // tc
        out_spec = pl.BlockSpec((tc, tl), lambda b, i, j: (j, b * nl + i))
        out_shape = jax.ShapeDtypeStruct((GN, T), BF)
    else:
        nx = DI // tc
        ncol = (DI + GN) // tc
        in_cb = lambda j: jnp.where(j < nx, j + C_X // tc, j - nx + C_C // tc)
        w_cb = lambda j: jnp.where(j < nx, j, j - nx + (C_C - C_X) // tc)
        out_spec = pl.BlockSpec((tl, tc), lambda b, i, j: (b * nl + i, j))
        out_shape = jax.ShapeDtypeStruct((T, DI + GN), BF)
    return pl.pallas_call(
        functools.partial(_conv_kernel, tl=tl, nl=nl, transpose=transpose),
        grid=(nb, nl, ncol),
        in_specs=[
            pl.BlockSpec((CH, tc), lambda b, i, j: (jnp.maximum((b * nl + i) * rb - 1, 0), in_cb(j))),
            pl.BlockSpec((tl, tc), lambda b, i, j: (b * nl + i, in_cb(j))),
            pl.BlockSpec((CH, tc), lambda b, i, j: (jnp.minimum((b * nl + i + 1) * rb, last), in_cb(j))),
            pl.BlockSpec(((KC - 1) * Q, 2 * Q), lambda b, i, j: (0, 0)),
            pl.BlockSpec((None, KC, tc), lambda b, i, j: (l, 0, w_cb(j))),
            pl.BlockSpec((None, 1, tc), lambda b, i, j: (l, 0, w_cb(j))),
        ],
        out_specs=out_spec,
        out_shape=out_shape,
        scratch_shapes=[pltpu.VMEM((tl + 2 * CH, tc), BF)],
        compiler_params=_params(("arbitrary", "arbitrary", "arbitrary")),
        name="conv_t" if transpose else "conv",
    )(main, main, main, shift, conv_w, conv_b)


def _tri(kind):
    r = lax.broadcasted_iota(jnp.int32, (Q, Q), 0)
    c = lax.broadcasted_iota(jnp.int32, (Q, Q), 1)
    return (c <= r) if kind == "le" else (c >= r)


def _expand_table():
    e = np.zeros((DTW, 2 * DI), np.float32)
    for k in range(2 * H):
        e[k, k * P:(k + 1) * P] = 1.0
    return np.concatenate([e, e], axis=0)


XROWS = 16


def _expand_lhs(w, dec):
    w_hi, w_lo = _split2(w)
    d_hi, d_mid, d_lo = [t.astype(F32) for t in _split3(jnp.broadcast_to(dec, (XROWS, DTW)))]
    rid = lax.broadcasted_iota(jnp.int32, (XROWS, DTW), 0)
    extra = jnp.where(rid == 0, d_hi, jnp.where(rid == 1, d_mid, jnp.where(rid == 2, d_lo, 0.0))).astype(BF)
    zero = jnp.zeros((XROWS, DTW), BF)
    return jnp.concatenate([jnp.concatenate([w_hi, w_lo], axis=1), jnp.concatenate([extra, zero], axis=1)], axis=0)


CPS = 2


def _ssd_state_kernel(*refs, nc, has_init):
    if has_init:
        (xf_ref, btf_ref, dtf_ref, xb_ref, btb_ref, dtb_ref, init_ref, alr_ref, e2_ref,
         entf_ref, entb_ref, fin_ref, st_ref, xw_ref, dec_ref) = refs
    else:
        (xf_ref, btf_ref, dtf_ref, xb_ref, btb_ref, dtb_ref, alr_ref, e2_ref,
         entf_ref, entb_ref, fin_ref, st_ref, xw_ref, dec_ref) = refs
    c = pl.program_id(1)

    @pl.when(c == 0)
    def _():
        st_ref[...] = init_ref[...] if has_init else jnp.zeros_like(st_ref)

    a_row = -jnp.exp(alr_ref[...])
    dirs = ((xf_ref, btf_ref, dtf_ref, entf_ref), (xb_ref, btb_ref, dtb_ref, entb_ref))
    order = [[(s, s) for s in range(CPS)], [(s, CPS - 1 - s) for s in range(CPS)]]
    for d, (x_ref, bt_ref, dt_ref, ent_ref) in enumerate(dirs):
        for s, ck in order[d]:
            rows = slice(ck * Q, (ck + 1) * Q)
            dt = dt_ref[rows, :]
            acum = _dot3_rhs(_tri("le" if d == 0 else "ge").astype(BF), dt * a_row)
            tot = acum[Q - 1:Q] if d == 0 else acum[0:1]
            ex = _dot(_expand_lhs(dt * jnp.exp(tot - acum), jnp.exp(tot)), e2_ref[:, d * DI:(d + 1) * DI])
            dec_ref[s, d] = (ex[Q + 2:Q + 3] + ex[Q + 1:Q + 2]) + ex[Q:Q + 1]
            xw_ref[s, d] = (x_ref[rows, :].astype(F32) * ex[0:Q]).astype(BF)
    for s in range(CPS):
        for d, (x_ref, bt_ref, dt_ref, ent_ref) in enumerate(dirs):
            ck = order[d][s][1]
            for g in range(G):
                cols = slice(g * GW, (g + 1) * GW)
                upd = _dot(bt_ref[g * N:(g + 1) * N, ck * Q:(ck + 1) * Q], xw_ref[s, d, :, cols])
                st = st_ref[d, g]
                ent_ref[ck, g] = st.astype(BF)
                st_ref[d, g] = st * dec_ref[s, d, :, cols] + upd

    @pl.when(c == nc - 1)
    def _():
        fin_ref[...] = st_ref[...]


def _ssd_state(xc, bt, dt, init, alog_r, expand2, l, nb, L):
    nc = L // (CPS * Q)
    fw = lambda b, c: b * nc + c
    bw = lambda b, c: b * nc + (nc - 1 - c)
    has_init = init is not None
    chunk_specs = lambda ch: [
        pl.BlockSpec((CPS * Q, DI), lambda b, c: (ch(b, c), 0)),
        pl.BlockSpec((GN, CPS * Q), lambda b, c: (0, ch(b, c))),
        pl.BlockSpec((CPS * Q, DTW), lambda b, c: (ch(b, c), 0)),
    ]
    st_spec = pl.BlockSpec((None, 2, G, N, GW), lambda b, c: (b, 0, 0, 0, 0))
    in_specs = chunk_specs(fw) + chunk_specs(bw) + ([st_spec] if has_init else []) + [
        _layer_spec((1, DTW), l),
        pl.BlockSpec((2 * DTW, 2 * DI), lambda b, c: (0, 0)),
    ]
    args = (xc, bt, dt, xc, bt, dt) + ((init,) if has_init else ()) + (alog_r, expand2)
    return pl.pallas_call(
        functools.partial(_ssd_state_kernel, nc=nc, has_init=has_init),
        grid=(nb, nc),
        in_specs=in_specs,
        out_specs=[
            pl.BlockSpec((None, CPS, G, N, GW), lambda b, c: (b, c, 0, 0, 0)),
            pl.BlockSpec((None, CPS, G, N, GW), lambda b, c: (b, nc - 1 - c, 0, 0, 0)),
            st_spec,
        ],
        out_shape=[
            jax.ShapeDtypeStruct((nb, nc * CPS, G, N, GW), BF),
            jax.ShapeDtypeStruct((nb, nc * CPS, G, N, GW), BF),
            jax.ShapeDtypeStruct((nb, 2, G, N, GW), F32),
        ],
        scratch_shapes=[pltpu.VMEM((2, G, N, GW), F32), pltpu.VMEM((CPS, 2, Q, DI), BF),
                        pltpu.VMEM((CPS, 2, 1, DI), F32)],
        compiler_params=_params(("arbitrary", "arbitrary")),
        name="ssd_state",
    )(*args)


def _fill_blockdiag(bd_ref, xg):
    blk = lax.broadcasted_iota(jnp.int32, (Q, GW), 1) // P
    for r in range(R):
        bd_ref[r * Q:(r + 1) * Q, :] = jnp.where(blk == r, xg, jnp.zeros_like(xg))


def _ssd_out_kernel(x_ref, c_ref, bt_ref, dt_ref, dtt_ref, entf_ref, entb_ref,
                    alr_ref, alc_ref, dsk_ref, o_ref, bd_ref):
    a_row = -jnp.exp(alr_ref[...])
    a_col = -jnp.exp(alc_ref[...])
    dt = dt_ref[...]
    dtt = dtt_ref[...]
    log_dtt = jnp.log(dtt)
    neg_inf = jnp.float32(-jnp.inf)
    left = lax.broadcasted_iota(jnp.int32, (Q, 2 * P), 1) < P
    acum, rowv, mask = [], [], []
    for d in range(2):
        m = _tri("le" if d == 0 else "ge")
        mask.append(m)
        acum.append(_dot3_rhs(m.astype(BF), dt * a_row) * LOG2E)
        acum_t = _dot3_lhs(dtt * a_col, _tri("ge" if d == 0 else "le").astype(BF))
        rowv.append((log_dtt - acum_t) * LOG2E)
    ents = (entf_ref, entb_ref)
    dsum = dsk_ref[0:1, :] + dsk_ref[1:2, :]
    for g in range(G):
        cols = slice(g * GW, (g + 1) * GW)
        xg = x_ref[:, cols]
        cg = c_ref[:, g * N:(g + 1) * N]
        cb = _dot(cg, bt_ref[g * N:(g + 1) * N, :])
        _fill_blockdiag(bd_ref.at[g], xg)
        y = dsum[:, cols] * xg.astype(F32)
        for d in range(2):
            ms, e_x = [], []
            for r in range(0, R, 2):
                es = []
                for k in (d * H + g * R + r, d * H + g * R + r + 1):
                    acol = jnp.broadcast_to(acum[d][:, k:k + 1], (Q, Q))
                    expo = acol + rowv[d][k:k + 1, :]
                    ms.append((cb * jnp.exp2(jnp.where(mask[d], expo, neg_inf))).astype(BF))
                    es.append(jnp.exp2(acol))
                e_x.append(jnp.where(left, es[0], es[1]))
            y = y + _dot(jnp.concatenate(ms, axis=1), bd_ref[g])
            y = y + _dot(cg, ents[d][g]) * jnp.concatenate(e_x, axis=1)
        o_ref[:, cols] = y.astype(BF)


def _ssd_out(xc, bt, dt, dtt, entf, entb, alog_r, alog_c, dskip_x, l, nb, L):
    nc = L // Q
    T = nb * L
    ch = lambda b, c: b * nc + c
    ent_spec = pl.BlockSpec((None, None, G, N, GW), lambda b, c: (b, c, 0, 0, 0))
    return pl.pallas_call(
        _ssd_out_kernel,
        grid=(nb, nc),
        in_specs=[
            pl.BlockSpec((Q, DI), lambda b, c: (ch(b, c), 0)),
            pl.BlockSpec((Q, GN), lambda b, c: (ch(b, c), DI // GN)),
            pl.BlockSpec((GN, Q), lambda b, c: (0, ch(b, c))),
            pl.BlockSpec((Q, DTW), lambda b, c: (ch(b, c), 0)),
            pl.BlockSpec((DTW, Q), lambda b, c: (0, ch(b, c))),
            ent_spec,
            ent_spec,
            _layer_spec((1, DTW), l),
            _layer_spec((DTW, 1), l),
            _layer_spec((2, DI), l),
        ],
        out_specs=pl.BlockSpec((Q, DI), lambda b, c: (ch(b, c), 0)),
        out_shape=jax.ShapeDtypeStruct((T, DI), BF),
        scratch_shapes=[pltpu.VMEM((G, R * Q, GW), BF)],
        compiler_params=_params(("parallel", "parallel")),
        name="ssd_out",
    )(xc, xc, bt, dt, dtt, entf, entb, alog_r, alog_c, dskip_x)


TM_MIX = 256
MIX_SUB = 2


def _mixer_kernel(y_ref, z_ref, u_ref, gs_ref, gp_ref, x_ref, ga_ref, nw_ref, band_ref, cnt_ref, wso_ref,
                  pw_ref, psc_ref, wpo_ref, wo_ref, o_ref):
    for s in range(MIX_SUB):
        rows = slice(s * TM_MIX, (s + 1) * TM_MIX)
        yns = []
        for g in range(G):
            cols = slice(g * GW, (g + 1) * GW)
            yz = y_ref[rows, cols].astype(F32) * _silu(z_ref[rows, cols].astype(F32))
            yn = yz * lax.rsqrt(jnp.mean(yz * yz, axis=-1, keepdims=True) + EPS)
            yns.append((yn * nw_ref[:, cols]).astype(BF))
        o_ssd = _dot(jnp.concatenate(yns, axis=1), wso_ref[...])
        pms = []
        for gi in range(NPW):
            ug = u_ref[rows, gi * PG:(gi + 1) * PG]
            wsum = _dot(band_ref[gi], ug)
            pm = wsum / cnt_ref[:, gi:gi + 1] - ug.astype(F32)
            pms.append(_dot(pm.astype(BF), pw_ref[gi]))
        pmc = (jnp.concatenate(pms, axis=1) * psc_ref[...]).astype(BF)
        o_pool = _dot(pmc, wpo_ref[...])
        mix = (jax.nn.sigmoid(gs_ref[rows, :].astype(F32)) * o_ssd
               + jax.nn.sigmoid(gp_ref[rows, :].astype(F32)) * o_pool)
        o_ref[rows, :] = x_ref[rows, :] + ga_ref[...] * _dot(mix.astype(BF), wo_ref[...])


def _pool_tables(seg):
    t = np.arange(TM_MIX)
    same = (t[:, None] // seg) == (t[None, :] // seg)
    ts = t % seg
    bands, cnts = [], []
    for k in POOL_WINDOWS:
        lo = t[:, None] - k // 2
        hi = t[:, None] + k // 2
        bands.append(same & (t[None, :] >= lo) & (t[None, :] < hi))
        cnts.append(np.minimum(ts + k // 2, seg) - np.maximum(ts - k // 2, 0))
    cnt = np.ones((TM_MIX, 128), np.float32)
    cnt[:, :NPW] = np.stack(cnts, axis=1)
    return jnp.asarray(np.stack(bands).astype(np.float32), BF), jnp.asarray(cnt)


def _mixer_out(y, main, x2, mod3, l, row_fn, seg, norm_w, w_ssd_out, pool_w, pool_scale, w_pool_out, w_out):
    T = x2.shape[0]
    tm = TM_MIX * MIX_SUB
    band, cnt = _pool_tables(seg)
    return pl.pallas_call(
        _mixer_kernel,
        grid=(T // tm,),
        in_specs=[
            pl.BlockSpec((tm, DI), lambda i: (i, 0)),
            pl.BlockSpec((tm, DI), lambda i: (i, C_Z // DI)),
            pl.BlockSpec((tm, D), lambda i: (i, C_POOL // D)),
            pl.BlockSpec((tm, D), lambda i: (i, C_GS // D)),
            pl.BlockSpec((tm, D), lambda i: (i, C_GP // D)),
            pl.BlockSpec((tm, D), lambda i: (i, 0)),
            _mod_spec(l, 2, row_fn),
            _layer_spec((1, DI), l),
            pl.BlockSpec((NPW, TM_MIX, TM_MIX), lambda i: (0, 0, 0)),
            pl.BlockSpec((TM_MIX, 128), lambda i: (0, 0)),
            _layer_spec((DI, D), l),
            _layer_spec((NPW, PG, PG), l),
            _layer_spec((1, D), l),
            _layer_spec((D, D), l),
            _layer_spec((D, D), l),
        ],
        out_specs=pl.BlockSpec((tm, D), lambda i: (i, 0)),
        out_shape=jax.ShapeDtypeStruct((T, D), F32),
        compiler_params=_params(("parallel",)),
        name="mixer_out",
    )(y, main, main, main, main, x2, mod3, norm_w, band, cnt, w_ssd_out, pool_w, pool_scale, w_pool_out, w_out)


FFN_CHUNKS = (768, 768, 768, 512)


def _ffn_kernel(x_ref, xn_ref, g_ref, sh_ref, sc_ref, ga_ref, shn_ref, scn_ref, wg_ref, wu_ref, wd_ref, gf_ref,
                o_ref, h_ref, *, final):
    i = pl.program_id(0)
    slot = i % 2

    def norm_mod(x, sh, sc):
        y = x * lax.rsqrt(jnp.mean(x * x, axis=-1, keepdims=True) + EPS)
        return ((y * g_ref[...]) * (1.0 + sc) + sh).astype(BF)

    @pl.when(i == 0)
    def _():
        h_ref[0] = norm_mod(x_ref[...], sh_ref[...], sc_ref[...])

    h_ref[1 - slot] = norm_mod(xn_ref[...], shn_ref[...], scn_ref[...])
    h = h_ref[slot]
    acc = None
    c0 = 0
    for cw in FFN_CHUNKS:
        act = (_silu(_dot(h, wg_ref[:, c0:c0 + cw])) * _dot(h, wu_ref[:, c0:c0 + cw])).astype(BF)
        part = _dot(act, wd_ref[c0:c0 + cw, :])
        acc = part if acc is None else acc + part
        c0 += cw
    xn = x_ref[...] + ga_ref[...] * acc
    if final:
        xn = (xn * lax.rsqrt(jnp.mean(xn * xn, axis=-1, keepdims=True) + EPS)) * gf_ref[...]
    o_ref[...] = xn


def _ffn(x2, g, mod3, l, row_fn, w_gate_up, w_down, g_final, tm, final):
    assert sum(FFN_CHUNKS) == FH
    T = x2.shape[0]
    n = T // tm
    nxt = lambda i: jnp.minimum(i + 1, n - 1)
    once = pl.Buffered(1)
    return pl.pallas_call(
        functools.partial(_ffn_kernel, final=final),
        grid=(n,),
        in_specs=[
            pl.BlockSpec((tm, D), lambda i: (i, 0)),
            pl.BlockSpec((tm, D), lambda i: (nxt(i), 0)),
            _layer_spec((1, D), l),
            _mod_spec(l, 3, row_fn),
            _mod_spec(l, 4, row_fn),
            _mod_spec(l, 5, row_fn),
            _mod_spec(l, 3, lambda i: row_fn(nxt(i))),
            _mod_spec(l, 4, lambda i: row_fn(nxt(i))),
            pl.BlockSpec((None, D, FH), lambda i: (l, 0, 0), pipeline_mode=once),
            pl.BlockSpec((None, D, FH), lambda i: (l, 0, 1), pipeline_mode=once),
            pl.BlockSpec((None, FH, D), lambda i: (l, 0, 0), pipeline_mode=once),
            pl.BlockSpec((1, D), lambda i: (0, 0)),
        ],
        out_specs=pl.BlockSpec((tm, D), lambda i: (i, 0)),
        out_shape=jax.ShapeDtypeStruct((T, D), F32),
        scratch_shapes=[pltpu.VMEM((2, tm, D), BF)],
        compiler_params=_params(("arbitrary",)),
        name="ffn",
    )(x2, x2, g, mod3, mod3, mod3, mod3, mod3, w_gate_up, w_gate_up, w_down, g_final)


def kernel(x, c, ctx, c_ctx, w_ada, b_ada, g_mix, w_in, conv_w, conv_b, dt_bias, a_log, d_skip, ssd_norm_w,
           w_ssd_out, pool_w, pool_scale, w_pool_out, w_out, g_ffn, w_gate_up, w_down, g_final):
    nb, L, _ = x.shape
    Lc = ctx.shape[1]
    ctx_row = nb

    cond8 = jnp.zeros((8, D), F32).at[:nb].set(c).at[ctx_row].set(c_ctx)
    mod3 = _adaln(cond8, w_ada, b_ada).reshape(DEPTH * 8 * 6, 1, D)

    shift = jnp.asarray(_shift_table(), BF)
    expand2 = jnp.asarray(_expand_table(), BF)

    w_in_t = jnp.swapaxes(w_in, 1, 2)
    w_main = jnp.concatenate([w_in_t[:, :C_POOL], w_in_t[:, C_POOL + 2 * H:]], axis=1).astype(BF)
    w_dt = jnp.pad(w_in_t[:, C_POOL:C_POOL + 2 * H], ((0, 0), (0, DTW - 2 * H), (0, 0)))
    pad_heads = lambda a: jnp.pad(a.reshape(DEPTH, 1, 2 * H), ((0, 0), (0, 0), (0, DTW - 2 * H)))
    dtb = pad_heads(dt_bias)
    alog_r = pad_heads(a_log)
    alog_c = alog_r.reshape(DEPTH, DTW, 1)
    dskip_x = jnp.repeat(d_skip, P, axis=2)
    norm_w = ssd_norm_w.reshape(DEPTH, 1, DI)
    gm = g_mix.reshape(DEPTH, 1, D)
    gf = g_ffn.reshape(DEPTH, 1, D)
    cbias = conv_b.reshape(DEPTH, 1, -1)
    wso = w_ssd_out.astype(BF)
    pw = pool_w.astype(BF)
    psc = pool_scale.reshape(DEPTH, 1, D)
    wpo = w_pool_out.astype(BF)
    wo = w_out.astype(BF)
    wgu = w_gate_up.astype(BF)
    wdn = w_down.astype(BF)
    gfin = g_final.reshape(1, D)

    xl = x.reshape(nb * L, D)
    xc_ = ctx.reshape(nb * Lc, D)
    tm_l = 1024
    tm_f = 512
    for l in range(DEPTH):
        last = l == DEPTH - 1

        def prepare(x2, row_fn, tm, seq):
            main, dt, dtt = _in_proj(x2, gm, mod3, l, row_fn, w_main, w_dt, dtb, tm)
            xcv = _conv(main, shift, conv_w, cbias, l, nb, seq, transpose=False)
            bt = _conv(main, shift, conv_w, cbias, l, nb, seq, transpose=True)
            return main, dt, dtt, xcv, bt

        def mix_and_ffn(x2, parts, ent, row_fn_mix, row_fn_ffn, seq, seg, final):
            main, dt, dtt, xcv, bt = parts
            y = _ssd_out(xcv, bt, dt, dtt, ent[0], ent[1], alog_r, alog_c, dskip_x, l, nb, seq)
            x2 = _mixer_out(y, main, x2, mod3, l, row_fn_mix, seg, norm_w, wso, pw, psc, wpo, wo)
            return _ffn(x2, gf, mod3, l, row_fn_ffn, wgu, wdn, gfin, tm_f, final)

        cparts = prepare(xc_, lambda i: ctx_row, 1024, Lc)
        centf, centb, cfin = _ssd_state(cparts[3], cparts[4], cparts[1], None, alog_r, expand2, l, nb, Lc)
        lparts = prepare(xl, lambda i: i // (L // tm_l), tm_l, L)
        lentf, lentb, _ = _ssd_state(lparts[3], lparts[4], lparts[1], cfin, alog_r, expand2, l, nb, L)
        xl = mix_and_ffn(xl, lparts, (lentf, lentb), lambda i: i // (L // (TM_MIX * MIX_SUB)),
                         lambda i: i // (L // tm_f), L, GRID_W, last)
        if not last:
            xc_ = mix_and_ffn(xc_, cparts, (centf, centb), lambda i: ctx_row, lambda i: ctx_row, Lc, Lc, False)
    return xl.reshape(nb, L, D)
```

```python
import functools

import numpy as np
import jax
import jax.numpy as jnp
from jax import lax
from jax.experimental import pallas as pl
from jax.experimental.pallas import tpu as pltpu

F32 = jnp.float32
BF = jnp.bfloat16

D = 1024
DEPTH = 2
H = 32
P = 64
G = 8
R = H // G
N = 128
Q = 128
DI = H * P
GN = G * N
GW = R * P
KC = 5
FH = 2816
POOL_WINDOWS = (2, 4, 8, 16)
NPW = len(POOL_WINDOWS)
PG = D // NPW
GRID_W = 64
EPS = 1e-6
LOG2E = 1.4426950408889634
DTW = 128

C_Z, C_X, C_B, C_C, C_POOL, C_GS, C_GP = 0, 2048, 4096, 5120, 6144, 7168, 8192
MAIN_W = 9216

VMEM_LIMIT = 56 * 1024 * 1024


def _dot(a, b):
    return jnp.dot(a, b, preferred_element_type=F32)


def _dot_nt(a, b):
    return lax.dot_general(a, b, (((1,), (1,)), ((), ())), preferred_element_type=F32)


def _split2(a):
    hi = a.astype(BF)
    return hi, (a - hi.astype(F32)).astype(BF)


def _split3(a):
    hi = a.astype(BF)
    r = a - hi.astype(F32)
    mid = r.astype(BF)
    lo = (r - mid.astype(F32)).astype(BF)
    return hi, mid, lo


def _dot3_rhs(m, a):
    hi, mid, lo = _split3(a)
    return (_dot(m, lo) + _dot(m, mid)) + _dot(m, hi)


def _dot3_lhs(a, m):
    hi, mid, lo = _split3(a)
    return (_dot(lo, m) + _dot(mid, m)) + _dot(hi, m)


def _silu(v):
    return v * jax.nn.sigmoid(v)


def _params(sem):
    return pltpu.CompilerParams(dimension_semantics=sem, vmem_limit_bytes=VMEM_LIMIT)


def _mod_spec(l, k, row_fn):
    return pl.BlockSpec((None, 1, D), lambda *ids: ((l * 8 + row_fn(*ids)) * 6 + k, 0, 0))


def _layer_spec(shape, l):
    zeros = (0,) * len(shape)
    return pl.BlockSpec((None,) + tuple(shape), lambda *ids: (l,) + zeros)


def _adaln_kernel(c_ref, w_ref, b_ref, o_ref):
    s_hi, s_lo = _split2(_silu(c_ref[...]))
    w_hi, w_lo = _split2(w_ref[...])
    acc = (_dot(s_lo, w_hi) + _dot(s_hi, w_lo)) + _dot(s_hi, w_hi)
    o_ref[...] = acc + b_ref[...]


def _adaln(cond8, w_ada, b_ada):
    tn = 1536
    return pl.pallas_call(
        _adaln_kernel,
        grid=(DEPTH, 6 * D // tn),
        in_specs=[
            pl.BlockSpec((8, D), lambda l, j: (0, 0)),
            pl.BlockSpec((None, D, tn), lambda l, j: (l, 0, j)),
            pl.BlockSpec((None, 1, tn), lambda l, j: (l, 0, j)),
        ],
        out_specs=pl.BlockSpec((None, 8, tn), lambda l, j: (l, 0, j)),
        out_shape=jax.ShapeDtypeStruct((DEPTH, 8, 6 * D), F32),
        compiler_params=_params(("arbitrary", "arbitrary")),
        name="adaln",
    )(cond8, w_ada, b_ada.reshape(DEPTH, 1, 6 * D))


def _inproj_kernel(x_ref, xn_ref, g_ref, sh_ref, sc_ref, shn_ref, scn_ref, w_ref, wdt_ref, dtb_ref,
                   o_ref, dt_ref, dtt_ref, h_ref, dts_ref, *, slab, nj):
    i = pl.program_id(0)
    j = pl.program_id(1)
    slot = i % 2
    tm, tn = o_ref.shape
    wh, wl = _split2(wdt_ref[...])
    wcat = jnp.concatenate([wh, wl], axis=0)

    def prologue(x, sh, sc):
        y = x * lax.rsqrt(jnp.mean(x * x, axis=-1, keepdims=True) + EPS)
        h = (y * g_ref[...]) * (1.0 + sc) + sh
        hb, hl = _split2(h)
        p = _dot_nt(hb, wcat)
        d = (_dot_nt(hl, wh) + p[:, DTW:]) + p[:, :DTW] + dtb_ref[...]
        return hb, jnp.maximum(d, 0.0) + jnp.log1p(jnp.exp(-jnp.abs(d)))

    @pl.when((i == 0) & (j == 0))
    def _():
        hb, dt = prologue(x_ref[...], sh_ref[...], sc_ref[...])
        h_ref[0] = hb
        dts_ref[0] = dt

    @pl.when(j == 0)
    def _():
        dt = dts_ref[slot]
        dt_ref[...] = dt
        dtt_ref[...] = dt.T

    def step(jj):
        rows = slice(min(jj * slab, tm - slab), min(jj * slab, tm - slab) + slab)
        hb, dt = prologue(xn_ref[rows, :], shn_ref[...], scn_ref[...])
        h_ref[1 - slot, rows, :] = hb
        dts_ref[1 - slot, rows, :] = dt
        v = _dot_nt(h_ref[slot], w_ref[0])
        lo, hi = jj * tn, (jj + 1) * tn
        n_silu = min(max(C_X - lo, 0), tn)
        n_sig = min(max(hi - C_GS, 0), tn)
        if n_silu:
            o_ref[:, :n_silu] = _silu(v[:, :n_silu]).astype(BF)
        if tn - n_silu - n_sig:
            o_ref[:, n_silu:tn - n_sig] = v[:, n_silu:tn - n_sig].astype(BF)
        if n_sig:
            o_ref[:, tn - n_sig:] = jax.nn.sigmoid(v[:, tn - n_sig:]).astype(BF)

    for jj in range(nj):
        pl.when(j == jj)(functools.partial(step, jj))


def _in_proj(x2, g, mod3, l, row_fn, w_main, w_dt, dt_bias, tm):
    T = x2.shape[0]
    n = T // tm
    tn = 3072
    nj = MAIN_W // tn
    slab = -(-tm // (nj * Q)) * Q
    nlow = C_POOL // tn
    w_row = lambda j: pl.multiple_of(jnp.where(j < nlow, j * tn, j * tn + 2 * H), 2 * H)
    nxt = lambda i: jnp.minimum(i + 1, n - 1)
    return pl.pallas_call(
        functools.partial(_inproj_kernel, slab=slab, nj=nj),
        grid=(n, nj),
        in_specs=[
            pl.BlockSpec((tm, D), lambda i, j: (i, 0)),
            pl.BlockSpec((tm, D), lambda i, j: (nxt(i), 0)),
            _layer_spec((1, D), l),
            _mod_spec(l, 0, lambda i, j: row_fn(i)),
            _mod_spec(l, 1, lambda i, j: row_fn(i)),
            _mod_spec(l, 0, lambda i, j: row_fn(nxt(i))),
            _mod_spec(l, 1, lambda i, j: row_fn(nxt(i))),
            pl.BlockSpec((pl.Element(1), pl.Element(tn), pl.Element(D)), lambda i, j: (l, w_row(j), 0)),
            _layer_spec((DTW, D), l),
            _layer_spec((1, DTW), l),
        ],
        out_specs=[
            pl.BlockSpec((tm, tn), lambda i, j: (i, j)),
            pl.BlockSpec((tm, DTW), lambda i, j: (i, 0)),
            pl.BlockSpec((DTW, tm), lambda i, j: (0, i)),
        ],
        out_shape=[
            jax.ShapeDtypeStruct((T, MAIN_W), BF),
            jax.ShapeDtypeStruct((T, DTW), F32),
            jax.ShapeDtypeStruct((DTW, T), F32),
        ],
        scratch_shapes=[pltpu.VMEM((2, tm, D), BF), pltpu.VMEM((2, tm, DTW), F32)],
        compiler_params=_params(("arbitrary", "arbitrary")),
        name="in_proj",
    )(x2, x2, g, mod3, mod3, mod3, mod3, w_main, w_dt, dt_bias)


CH = 64


SUBL = 8
CONV_SUB = 256


def _shift_table():
    s = np.zeros((Q // SUBL, KC - 1, SUBL, 2 * Q), np.float32)
    t = np.arange(Q)
    for si, k in enumerate([k for k in range(KC) if k != KC // 2]):
        s[t // SUBL, si, t % SUBL, CH + t + k - KC // 2] = 1.0
    return s.reshape((KC - 1) * Q, 2 * Q)


def _conv_kernel(prev_ref, cur_ref, next_ref, s_ref, w_ref, b_ref, o_ref, ext_ref, *, tl, nl, transpose):
    i = pl.program_id(1)
    prev = prev_ref[...]
    nxt = next_ref[...]
    ext_ref[0:CH, :] = jnp.where(i > 0, prev, jnp.zeros_like(prev))
    ext_ref[CH:CH + tl, :] = cur_ref[...]
    ext_ref[CH + tl:2 * CH + tl, :] = jnp.where(i < nl - 1, nxt, jnp.zeros_like(nxt))
    tc = cur_ref.shape[1]
    for r in range(tl // Q):
        for c0 in range(0, tc, CONV_SUB):
            cols = slice(c0, c0 + CONV_SUB)
            win = ext_ref[r * Q:(r + 2) * Q, cols]
            sh = _dot(s_ref[...], win)
            acc = b_ref[:, cols]
            si = 0
            for k in range(KC):
                if k == KC // 2:
                    tap = win[CH:CH + Q, :].astype(F32)
                else:
                    starts = [(tg * (KC - 1) + si) * SUBL for tg in range(Q // SUBL)]
                    tap = jnp.concatenate([sh[a:a + SUBL, :] for a in starts], axis=0)
                    si += 1
                acc = acc + w_ref[k:k + 1, cols] * tap
            out = _silu(acc)
            if transpose:
                o_ref[cols, r * Q:(r + 1) * Q] = out.T.astype(BF)
            else:
                o_ref[r * Q:(r + 1) * Q, cols] = out.astype(BF)


def _conv(main, shift, conv_w, conv_b, l, nb, L, transpose):
    T = nb * L
    tc = 1024
    tl = min(L, 512)
    nl = L // tl
    rb = tl // CH
    last = T // CH - 1
    if transpose:
        ncol = GN // tc
        in_cb = lambda j: j + C_B // tc
        w_cb = lambda j: j + (C_B - C_X) // tc
        out_spec = pl.BlockSpec((tc, tl), lambda b, i, j: (j, b * nl + i))
        out_shape = jax.ShapeDtypeStruct((GN, T), BF)
    else:
        nx = DI // tc
        ncol = (DI + GN) // tc
        in_cb = lambda j: jnp.where(j < nx, j + C_X // tc, j - nx + C_C // tc)
        w_cb = lambda j: jnp.where(j < nx, j, j - nx + (C_C - C_X) // tc)
        out_spec = pl.BlockSpec((tl, tc), lambda b, i, j: (b * nl + i, j))
        out_shape = jax.ShapeDtypeStruct((T, DI + GN), BF)
    return pl.pallas_call(
        functools.partial(_conv_kernel, tl=tl, nl=nl, transpose=transpose),
        grid=(nb, nl, ncol),
        in_specs=[
            pl.BlockSpec((CH, tc), lambda b, i, j: (jnp.maximum((b * nl + i) * rb - 1, 0), in_cb(j))),
            pl.BlockSpec((tl, tc), lambda b, i, j: (b * nl + i, in_cb(j))),
            pl.BlockSpec((CH, tc), lambda b, i, j: (jnp.minimum((b * nl + i + 1) * rb, last), in_cb(j))),
            pl.BlockSpec(((KC - 1) * Q, 2 * Q), lambda b, i, j: (0, 0)),
            pl.BlockSpec((None, KC, tc), lambda b, i, j: (l, 0, w_cb(j))),
            pl.BlockSpec((None, 1, tc), lambda b, i, j: (l, 0, w_cb(j))),
        ],
        out_specs=out_spec,
        out_shape=out_shape,
        scratch_shapes=[pltpu.VMEM((tl + 2 * CH, tc), BF)],
        compiler_params=_params(("arbitrary", "arbitrary", "arbitrary")),
        name="conv_t" if transpose else "conv",
    )(main, main, main, shift, conv_w, conv_b)


def _tri(kind):
    r = lax.broadcasted_iota(jnp.int32, (Q, Q), 0)
    c = lax.broadcasted_iota(jnp.int32, (Q, Q), 1)
    return (c <= r) if kind == "le" else (c >= r)


def _expand_table():
    e = np.zeros((DTW, 2 * DI), np.float32)
    for k in range(2 * H):
        e[k, k * P:(k + 1) * P] = 1.0
    return np.concatenate([e, e], axis=0)


XROWS = 16


def _expand_lhs(w, dec):
    w_hi, w_lo = _split2(w)
    d_hi, d_mid, d_lo = [t.astype(F32) for t in _split3(jnp.broadcast_to(dec, (XROWS, DTW)))]
    rid = lax.broadcasted_iota(jnp.int32, (XROWS, DTW), 0)
    extra = jnp.where(rid == 0, d_hi, jnp.where(rid == 1, d_mid, jnp.where(rid == 2, d_lo, 0.0))).astype(BF)
    zero = jnp.zeros((XROWS, DTW), BF)
    return jnp.concatenate([jnp.concatenate([w_hi, w_lo], axis=1), jnp.concatenate([extra, zero], axis=1)], axis=0)


MAX_CPS = 4


def _ssd_state_kernel(*refs, nc, has_init):
    CPS = refs[-1].shape[0]
    if has_init:
        (xf_ref, btf_ref, dtf_ref, xb_ref, btb_ref, dtb_ref, init_ref, alr_ref, e2_ref,
         entf_ref, entb_ref, fin_ref, st_ref, xw_ref, dec_ref) = refs
    else:
        (xf_ref, btf_ref, dtf_ref, xb_ref, btb_ref, dtb_ref, alr_ref, e2_ref,
         entf_ref, entb_ref, fin_ref, st_ref, xw_ref, dec_ref) = refs
    c = pl.program_id(1)

    @pl.when(c == 0)
    def _():
        st_ref[...] = init_ref[...] if has_init else jnp.zeros_like(st_ref)

    a_row = -jnp.exp(alr_ref[...])
    dirs = ((xf_ref, btf_ref, dtf_ref, entf_ref), (xb_ref, btb_ref, dtb_ref, entb_ref))
    order = [[(s, s) for s in range(CPS)], [(s, CPS - 1 - s) for s in range(CPS)]]
    for d, (x_ref, bt_ref, dt_ref, ent_ref) in enumerate(dirs):
        for s, ck in order[d]:
            rows = slice(ck * Q, (ck + 1) * Q)
            dt = dt_ref[rows, :]
            acum = _dot3_rhs(_tri("le" if d == 0 else "ge").astype(BF), dt * a_row)
            tot = acum[Q - 1:Q] if d == 0 else acum[0:1]
            ex = _dot(_expand_lhs(dt * jnp.exp(tot - acum), jnp.exp(tot)), e2_ref[:, d * DI:(d + 1) * DI])
            dec_ref[s, d] = (ex[Q + 2:Q + 3] + ex[Q + 1:Q + 2]) + ex[Q:Q + 1]
            xw_ref[s, d] = (x_ref[rows, :].astype(F32) * ex[0:Q]).astype(BF)
    for s in range(CPS):
        for d, (x_ref, bt_ref, dt_ref, ent_ref) in enumerate(dirs):
            ck = order[d][s][1]
            for g in range(G):
                cols = slice(g * GW, (g + 1) * GW)
                upd = _dot(bt_ref[g * N:(g + 1) * N, ck * Q:(ck + 1) * Q], xw_ref[s, d, :, cols])
                st = st_ref[d, g]
                ent_ref[ck, g] = st.astype(BF)
                st_ref[d, g] = st * dec_ref[s, d, :, cols] + upd

    @pl.when(c == nc - 1)
    def _():
        fin_ref[...] = st_ref[...]


def _ssd_state(xc, bt, dt, init, alog_r, expand2, l, nb, L):
    CPS = min(MAX_CPS, L // Q)
    nc = L // (CPS * Q)
    fw = lambda b, c: b * nc + c
    bw = lambda b, c: b * nc + (nc - 1 - c)
    has_init = init is not None
    chunk_specs = lambda ch: [
        pl.BlockSpec((CPS * Q, DI), lambda b, c: (ch(b, c), 0)),
        pl.BlockSpec((GN, CPS * Q), lambda b, c: (0, ch(b, c))),
        pl.BlockSpec((CPS * Q, DTW), lambda b, c: (ch(b, c), 0)),
    ]
    st_spec = pl.BlockSpec((None, 2, G, N, GW), lambda b, c: (b, 0, 0, 0, 0))
    in_specs = chunk_specs(fw) + chunk_specs(bw) + ([st_spec] if has_init else []) + [
        _layer_spec((1, DTW), l),
        pl.BlockSpec((2 * DTW, 2 * DI), lambda b, c: (0, 0)),
    ]
    args = (xc, bt, dt, xc, bt, dt) + ((init,) if has_init else ()) + (alog_r, expand2)
    return pl.pallas_call(
        functools.partial(_ssd_state_kernel, nc=nc, has_init=has_init),
        grid=(nb, nc),
        in_specs=in_specs,
        out_specs=[
            pl.BlockSpec((None, CPS, G, N, GW), lambda b, c: (b, c, 0, 0, 0)),
            pl.BlockSpec((None, CPS, G, N, GW), lambda b, c: (b, nc - 1 - c, 0, 0, 0)),
            st_spec,
        ],
        out_shape=[
            jax.ShapeDtypeStruct((nb, nc * CPS, G, N, GW), BF),
            jax.ShapeDtypeStruct((nb, nc * CPS, G, N, GW), BF),
            jax.ShapeDtypeStruct((nb, 2, G, N, GW), F32),
        ],
        scratch_shapes=[pltpu.VMEM((2, G, N, GW), F32), pltpu.VMEM((CPS, 2, Q, DI), BF),
                        pltpu.VMEM((CPS, 2, 1, DI), F32)],
        compiler_params=_params(("arbitrary", "arbitrary")),
        name="ssd_state",
    )(*args)


OUT_CPS = 2


def _fill_blockdiag(bd_ref, xg):
    blk = lax.broadcasted_iota(jnp.int32, (Q, GW), 1) // P
    for r in range(R):
        bd_ref[r * Q:(r + 1) * Q, :] = jnp.where(blk == r, xg, jnp.zeros_like(xg))


def _ssd_out_kernel(x_ref, c_ref, bt_ref, dt_ref, dtt_ref, entf_ref, entb_ref,
                    alr_ref, alc_ref, dsk_ref, o_ref, bd_ref):
    a_row = -jnp.exp(alr_ref[...])
    a_col = -jnp.exp(alc_ref[...])
    neg_inf = jnp.float32(-jnp.inf)
    left = lax.broadcasted_iota(jnp.int32, (Q, 2 * P), 1) < P
    mask = [_tri("le"), _tri("ge")]
    ents = (entf_ref, entb_ref)
    dsum = dsk_ref[0:1, :] + dsk_ref[1:2, :]
    for s in range(o_ref.shape[0] // Q):
        rows = slice(s * Q, (s + 1) * Q)
        dt = dt_ref[rows, :]
        dtt = dtt_ref[:, rows]
        log_dtt = jnp.log(dtt)
        acum, rowv = [], []
        for d in range(2):
            acum.append(_dot3_rhs(mask[d].astype(BF), dt * a_row) * LOG2E)
            acum_t = _dot3_lhs(dtt * a_col, mask[1 - d].astype(BF))
            rowv.append((log_dtt - acum_t) * LOG2E)
        for g in range(G):
            cols = slice(g * GW, (g + 1) * GW)
            xg = x_ref[rows, cols]
            cg = c_ref[rows, g * N:(g + 1) * N]
            cb = _dot(cg, bt_ref[g * N:(g + 1) * N, rows])
            _fill_blockdiag(bd_ref.at[s, g], xg)
            y = dsum[:, cols] * xg.astype(F32)
            for d in range(2):
                ms, e_x = [], []
                for r in range(0, R, 2):
                    es = []
                    for k in (d * H + g * R + r, d * H + g * R + r + 1):
                        acol = jnp.broadcast_to(acum[d][:, k:k + 1], (Q, Q))
                        expo = acol + rowv[d][k:k + 1, :]
                        ms.append((cb * jnp.exp2(jnp.where(mask[d], expo, neg_inf))).astype(BF))
                        es.append(jnp.exp2(acol))
                    e_x.append(jnp.where(left, es[0], es[1]))
                y = y + _dot(jnp.concatenate(ms, axis=1), bd_ref[s, g])
                y = y + _dot(cg, ents[d][s, g]) * jnp.concatenate(e_x, axis=1)
            o_ref[rows, cols] = y.astype(BF)


def _ssd_out(xc, bt, dt, dtt, entf, entb, alog_r, alog_c, dskip_x, l, nb, L):
    cps = OUT_CPS
    rows = cps * Q
    nc = L // rows
    T = nb * L
    ch = lambda b, c: b * nc + c
    ent_spec = pl.BlockSpec((None, cps, G, N, GW), lambda b, c: (b, c, 0, 0, 0))
    return pl.pallas_call(
        _ssd_out_kernel,
        grid=(nb, nc),
        in_specs=[
            pl.BlockSpec((rows, DI), lambda b, c: (ch(b, c), 0)),
            pl.BlockSpec((rows, GN), lambda b, c: (ch(b, c), DI // GN)),
            pl.BlockSpec((GN, rows), lambda b, c: (0, ch(b, c))),
            pl.BlockSpec((rows, DTW), lambda b, c: (ch(b, c), 0)),
            pl.BlockSpec((DTW, rows), lambda b, c: (0, ch(b, c))),
            ent_spec,
            ent_spec,
            _layer_spec((1, DTW), l),
            _layer_spec((DTW, 1), l),
            _layer_spec((2, DI), l),
        ],
        out_specs=pl.BlockSpec((rows, DI), lambda b, c: (ch(b, c), 0)),
        out_shape=jax.ShapeDtypeStruct((T, DI), BF),
        scratch_shapes=[pltpu.VMEM((cps, G, R * Q, GW), BF)],
        compiler_params=_params(("parallel", "parallel")),
        name="ssd_out",
    )(xc, xc, bt, dt, dtt, entf, entb, alog_r, alog_c, dskip_x)


TM_MIX = 256
MIX_SUB = 4


def _mixer_kernel(y_ref, z_ref, u_ref, gs_ref, gp_ref, x_ref, ga_ref, nw_ref, band_ref, cnt_ref, wso_ref,
                  pw_ref, psc_ref, wpo_ref, wo_ref, o_ref):
    for s in range(MIX_SUB):
        rows = slice(s * TM_MIX, (s + 1) * TM_MIX)
        yns = []
        for g in range(G):
            cols = slice(g * GW, (g + 1) * GW)
            yz = y_ref[rows, cols].astype(F32) * z_ref[rows, cols].astype(F32)
            yn = yz * lax.rsqrt(jnp.mean(yz * yz, axis=-1, keepdims=True) + EPS)
            yns.append((yn * nw_ref[:, cols]).astype(BF))
        o_ssd = _dot(jnp.concatenate(yns, axis=1), wso_ref[...])
        pms = []
        for gi in range(NPW):
            ug = u_ref[rows, gi * PG:(gi + 1) * PG]
            wsum = _dot(band_ref[gi], ug)
            pm = wsum / cnt_ref[:, gi:gi + 1] - ug.astype(F32)
            pms.append(_dot(pm.astype(BF), pw_ref[gi]))
        pmc = (jnp.concatenate(pms, axis=1) * psc_ref[...]).astype(BF)
        o_pool = _dot(pmc, wpo_ref[...])
        mix = gs_ref[rows, :].astype(F32) * o_ssd + gp_ref[rows, :].astype(F32) * o_pool
        o_ref[rows, :] = x_ref[rows, :] + ga_ref[...] * _dot(mix.astype(BF), wo_ref[...])


def _pool_tables(seg):
    t = np.arange(TM_MIX)
    same = (t[:, None] // seg) == (t[None, :] // seg)
    ts = t % seg
    bands, cnts = [], []
    for k in POOL_WINDOWS:
        lo = t[:, None] - k // 2
        hi = t[:, None] + k // 2
        bands.append(same & (t[None, :] >= lo) & (t[None, :] < hi))
        cnts.append(np.minimum(ts + k // 2, seg) - np.maximum(ts - k // 2, 0))
    cnt = np.ones((TM_MIX, 128), np.float32)
    cnt[:, :NPW] = np.stack(cnts, axis=1)
    return jnp.asarray(np.stack(bands).astype(np.float32), BF), jnp.asarray(cnt)


def _mixer_out(y, main, x2, mod3, l, row_fn, seg, norm_w, w_ssd_out, pool_w, pool_scale, w_pool_out, w_out):
    T = x2.shape[0]
    tm = TM_MIX * MIX_SUB
    band, cnt = _pool_tables(seg)
    return pl.pallas_call(
        _mixer_kernel,
        grid=(T // tm,),
        in_specs=[
            pl.BlockSpec((tm, DI), lambda i: (i, 0)),
            pl.BlockSpec((tm, DI), lambda i: (i, C_Z // DI)),
            pl.BlockSpec((tm, D), lambda i: (i, C_POOL // D)),
            pl.BlockSpec((tm, D), lambda i: (i, C_GS // D)),
            pl.BlockSpec((tm, D), lambda i: (i, C_GP // D)),
            pl.BlockSpec((tm, D), lambda i: (i, 0)),
            _mod_spec(l, 2, row_fn),
            _layer_spec((1, DI), l),
            pl.BlockSpec((NPW, TM_MIX, TM_MIX), lambda i: (0, 0, 0)),
            pl.BlockSpec((TM_MIX, 128), lambda i: (0, 0)),
            _layer_spec((DI, D), l),
            _layer_spec((NPW, PG, PG), l),
            _layer_spec((1, D), l),
            _layer_spec((D, D), l),
            _layer_spec((D, D), l),
        ],
        out_specs=pl.BlockSpec((tm, D), lambda i: (i, 0)),
        out_shape=jax.ShapeDtypeStruct((T, D), F32),
        compiler_params=_params(("parallel",)),
        name="mixer_out",
    )(y, main, main, main, main, x2, mod3, norm_w, band, cnt, w_ssd_out, pool_w, pool_scale, w_pool_out, w_out)


FFN_CHUNKS = (768, 768, 768, 512)


def _ffn_kernel(x_ref, xn_ref, g_ref, sh_ref, sc_ref, ga_ref, shn_ref, scn_ref, wg_ref, wu_ref, wd_ref, gf_ref,
                o_ref, h_ref, *, final):
    i = pl.program_id(0)
    slot = i % 2

    def norm_mod(x, sh, sc):
        y = x * lax.rsqrt(jnp.mean(x * x, axis=-1, keepdims=True) + EPS)
        return ((y * g_ref[...]) * (1.0 + sc) + sh).astype(BF)

    @pl.when(i == 0)
    def _():
        h_ref[0] = norm_mod(x_ref[...], sh_ref[...], sc_ref[...])

    h_ref[1 - slot] = norm_mod(xn_ref[...], shn_ref[...], scn_ref[...])
    h = h_ref[slot]
    acc = None
    c0 = 0
    for cw in FFN_CHUNKS:
        act = (_silu(_dot(h, wg_ref[:, c0:c0 + cw])) * _dot(h, wu_ref[:, c0:c0 + cw])).astype(BF)
        part = _dot(act, wd_ref[c0:c0 + cw, :])
        acc = part if acc is None else acc + part
        c0 += cw
    xn = x_ref[...] + ga_ref[...] * acc
    if final:
        xn = (xn * lax.rsqrt(jnp.mean(xn * xn, axis=-1, keepdims=True) + EPS)) * gf_ref[...]
    o_ref[...] = xn


def _ffn(x2, g, mod3, l, row_fn, w_gate_up, w_down, g_final, tm, final):
    assert sum(FFN_CHUNKS) == FH
    T = x2.shape[0]
    n = T // tm
    nxt = lambda i: jnp.minimum(i + 1, n - 1)
    once = pl.Buffered(1)
    return pl.pallas_call(
        functools.partial(_ffn_kernel, final=final),
        grid=(n,),
        in_specs=[
            pl.BlockSpec((tm, D), lambda i: (i, 0)),
            pl.BlockSpec((tm, D), lambda i: (nxt(i), 0)),
            _layer_spec((1, D), l),
            _mod_spec(l, 3, row_fn),
            _mod_spec(l, 4, row_fn),
            _mod_spec(l, 5, row_fn),
            _mod_spec(l, 3, lambda i: row_fn(nxt(i))),
            _mod_spec(l, 4, lambda i: row_fn(nxt(i))),
            pl.BlockSpec((None, D, FH), lambda i: (l, 0, 0), pipeline_mode=once),
            pl.BlockSpec((None, D, FH), lambda i: (l, 0, 1), pipeline_mode=once),
            pl.BlockSpec((None, FH, D), lambda i: (l, 0, 0), pipeline_mode=once),
            pl.BlockSpec((1, D), lambda i: (0, 0)),
        ],
        out_specs=pl.BlockSpec((tm, D), lambda i: (i, 0)),
        out_shape=jax.ShapeDtypeStruct((T, D), F32),
        scratch_shapes=[pltpu.VMEM((2, tm, D), BF)],
        compiler_params=_params(("arbitrary",)),
        name="ffn",
    )(x2, x2, g, mod3, mod3, mod3, mod3, mod3, w_gate_up, w_gate_up, w_down, g_final)


def kernel(x, c, ctx, c_ctx, w_ada, b_ada, g_mix, w_in, conv_w, conv_b, dt_bias, a_log, d_skip, ssd_norm_w,
           w_ssd_out, pool_w, pool_scale, w_pool_out, w_out, g_ffn, w_gate_up, w_down, g_final):
    nb, L, _ = x.shape
    Lc = ctx.shape[1]
    ctx_row = nb

    cond8 = jnp.zeros((8, D), F32).at[:nb].set(c).at[ctx_row].set(c_ctx)
    mod3 = _adaln(cond8, w_ada, b_ada).reshape(DEPTH * 8 * 6, 1, D)

    shift = jnp.asarray(_shift_table(), BF)
    expand2 = jnp.asarray(_expand_table(), BF)

    w_in_t = jnp.swapaxes(w_in, 1, 2)
    w_main = w_in_t.astype(BF)
    w_dt = jnp.pad(w_in_t[:, C_POOL:C_POOL + 2 * H], ((0, 0), (0, DTW - 2 * H), (0, 0)))
    pad_heads = lambda a: jnp.pad(a.reshape(DEPTH, 1, 2 * H), ((0, 0), (0, 0), (0, DTW - 2 * H)))
    dtb = pad_heads(dt_bias)
    alog_r = pad_heads(a_log)
    alog_c = alog_r.reshape(DEPTH, DTW, 1)
    dskip_x = jnp.repeat(d_skip, P, axis=2)
    norm_w = ssd_norm_w.reshape(DEPTH, 1, DI)
    gm = g_mix.reshape(DEPTH, 1, D)
    gf = g_ffn.reshape(DEPTH, 1, D)
    cbias = conv_b.reshape(DEPTH, 1, -1)
    wso = w_ssd_out.astype(BF)
    pw = pool_w.astype(BF)
    psc = pool_scale.reshape(DEPTH, 1, D)
    wpo = w_pool_out.astype(BF)
    wo = w_out.astype(BF)
    wgu = w_gate_up.astype(BF)
    wdn = w_down.astype(BF)
    gfin = g_final.reshape(1, D)

    xl = x.reshape(nb * L, D)
    xc_ = ctx.reshape(nb * Lc, D)
    tm_l = 1024
    tm_f = 512
    for l in range(DEPTH):
        last = l == DEPTH - 1

        def prepare(x2, row_fn, tm, seq):
            main, dt, dtt = _in_proj(x2, gm, mod3, l, row_fn, w_main, w_dt, dtb, tm)
            xcv = _conv(main, shift, conv_w, cbias, l, nb, seq, transpose=False)
            bt = _conv(main, shift, conv_w, cbias, l, nb, seq, transpose=True)
            return main, dt, dtt, xcv, bt

        def mix_and_ffn(x2, parts, ent, row_fn_mix, row_fn_ffn, seq, seg, final):
            main, dt, dtt, xcv, bt = parts
            y = _ssd_out(xcv, bt, dt, dtt, ent[0], ent[1], alog_r, alog_c, dskip_x, l, nb, seq)
            x2 = _mixer_out(y, main, x2, mod3, l, row_fn_mix, seg, norm_w, wso, pw, psc, wpo, wo)
            return _ffn(x2, gf, mod3, l, row_fn_ffn, wgu, wdn, gfin, tm_f, final)

        cparts = prepare(xc_, lambda i: ctx_row, 1024, Lc)
        centf, centb, cfin = _ssd_state(cparts[3], cparts[4], cparts[1], None, alog_r, expand2, l, nb, Lc)
        lparts = prepare(xl, lambda i: i // (L // tm_l), tm_l, L)
        lentf, lentb, _ = _ssd_state(lparts[3], lparts[4], lparts[1], cfin, alog_r, expand2, l, nb, L)
        xl = mix_and_ffn(xl, lparts, (lentf, lentb), lambda i: i // (L // (TM_MIX * MIX_SUB)),
                         lambda i: i // (L // tm_f), L, GRID_W, last)
        if not last:
            xc_ = mix_and_ffn(xc_, cparts, (centf, centb), lambda i: ctx_row, lambda i: ctx_row, Lc, Lc, False)
    return xl.reshape(nb, L, D)
```

```python
import functools

import numpy as np
import jax
import jax.numpy as jnp
from jax import lax
from jax.experimental import pallas as pl
from jax.experimental.pallas import tpu as pltpu

F32 = jnp.float32
BF = jnp.bfloat16

D = 1024
DEPTH = 2
H = 32
P = 64
G = 8
R = H // G
N = 128
Q = 128
DI = H * P
GN = G * N
GW = R * P
KC = 5
FH = 2816
POOL_WINDOWS = (2, 4, 8, 16)
NPW = len(POOL_WINDOWS)
PG = D // NPW
GRID_W = 64
EPS = 1e-6
LOG2E = 1.4426950408889634
DTW = 128

C_Z, C_X, C_B, C_C, C_POOL, C_GS, C_GP = 0, 2048, 4096, 5120, 6144, 7168, 8192
MAIN_W = 9216

VMEM_LIMIT = 56 * 1024 * 1024


def _dot(a, b):
    return jnp.dot(a, b, preferred_element_type=F32)


def _dot_nt(a, b):
    return lax.dot_general(a, b, (((1,), (1,)), ((), ())), preferred_element_type=F32)


def _split2(a):
    hi = a.astype(BF)
    return hi, (a - hi.astype(F32)).astype(BF)


def _split3(a):
    hi = a.astype(BF)
    r = a - hi.astype(F32)
    mid = r.astype(BF)
    lo = (r - mid.astype(F32)).astype(BF)
    return hi, mid, lo


def _dot3_rhs(m, a):
    hi, mid, lo = _split3(a)
    return (_dot(m, lo) + _dot(m, mid)) + _dot(m, hi)


def _dot3_lhs(a, m):
    hi, mid, lo = _split3(a)
    return (_dot(lo, m) + _dot(mid, m)) + _dot(hi, m)


def _silu(v):
    return v * jax.nn.sigmoid(v)


def _params(sem):
    return pltpu.CompilerParams(dimension_semantics=sem, vmem_limit_bytes=VMEM_LIMIT)


def _mod_spec(l, k, row_fn):
    return pl.BlockSpec((None, 1, D), lambda *ids: ((l * 8 + row_fn(*ids)) * 6 + k, 0, 0))


def _layer_spec(shape, l):
    zeros = (0,) * len(shape)
    return pl.BlockSpec((None,) + tuple(shape), lambda *ids: (l,) + zeros)


def _adaln_kernel(c_ref, w_ref, b_ref, o_ref):
    s_hi, s_lo = _split2(_silu(c_ref[...]))
    w_hi, w_lo = _split2(w_ref[...])
    acc = (_dot(s_lo, w_hi) + _dot(s_hi, w_lo)) + _dot(s_hi, w_hi)
    o_ref[...] = acc + b_ref[...]


def _adaln(cond8, w_ada, b_ada):
    tn = 1536
    return pl.pallas_call(
        _adaln_kernel,
        grid=(DEPTH, 6 * D // tn),
        in_specs=[
            pl.BlockSpec((8, D), lambda l, j: (0, 0)),
            pl.BlockSpec((None, D, tn), lambda l, j: (l, 0, j)),
            pl.BlockSpec((None, 1, tn), lambda l, j: (l, 0, j)),
        ],
        out_specs=pl.BlockSpec((None, 8, tn), lambda l, j: (l, 0, j)),
        out_shape=jax.ShapeDtypeStruct((DEPTH, 8, 6 * D), F32),
        compiler_params=_params(("arbitrary", "arbitrary")),
        name="adaln",
    )(cond8, w_ada, b_ada.reshape(DEPTH, 1, 6 * D))


def _inproj_kernel(x_ref, xn_ref, g_ref, sh_ref, sc_ref, shn_ref, scn_ref, w_ref, wdt_ref, dtb_ref,
                   o_ref, dt_ref, dtt_ref, h_ref, dts_ref, *, slab, nj):
    i = pl.program_id(0)
    j = pl.program_id(1)
    slot = i % 2
    tm, tn = o_ref.shape
    wh, wl = _split2(wdt_ref[...])
    wcat = jnp.concatenate([wh, wl], axis=0)

    def prologue(x, sh, sc):
        y = x * lax.rsqrt(jnp.mean(x * x, axis=-1, keepdims=True) + EPS)
        h = (y * g_ref[...]) * (1.0 + sc) + sh
        hb, hl = _split2(h)
        p = _dot_nt(hb, wcat)
        d = (_dot_nt(hl, wh) + p[:, DTW:]) + p[:, :DTW] + dtb_ref[...]
        return hb, jnp.maximum(d, 0.0) + jnp.log1p(jnp.exp(-jnp.abs(d)))

    @pl.when((i == 0) & (j == 0))
    def _():
        hb, dt = prologue(x_ref[...], sh_ref[...], sc_ref[...])
        h_ref[0] = hb
        dts_ref[0] = dt

    @pl.when(j == 0)
    def _():
        dt = dts_ref[slot]
        dt_ref[...] = dt
        dtt_ref[...] = dt.T

    def step(jj):
        rows = slice(min(jj * slab, tm - slab), min(jj * slab, tm - slab) + slab)
        hb, dt = prologue(xn_ref[rows, :], shn_ref[...], scn_ref[...])
        h_ref[1 - slot, rows, :] = hb
        dts_ref[1 - slot, rows, :] = dt
        v = _dot_nt(h_ref[slot], w_ref[0])
        lo, hi = jj * tn, (jj + 1) * tn
        n_silu = min(max(C_X - lo, 0), tn)
        n_sig = min(max(hi - C_GS, 0), tn)
        if n_silu:
            o_ref[:, :n_silu] = _silu(v[:, :n_silu]).astype(BF)
        if tn - n_silu - n_sig:
            o_ref[:, n_silu:tn - n_sig] = v[:, n_silu:tn - n_sig].astype(BF)
        if n_sig:
            o_ref[:, tn - n_sig:] = jax.nn.sigmoid(v[:, tn - n_sig:]).astype(BF)

    for jj in range(nj):
        pl.when(j == jj)(functools.partial(step, jj))


def _in_proj(x2, g, mod3, l, row_fn, w_main, w_dt, dt_bias, tm):
    T = x2.shape[0]
    n = T // tm
    tn = 3072
    nj = MAIN_W // tn
    slab = -(-tm // (nj * Q)) * Q
    nlow = C_POOL // tn
    w_row = lambda j: pl.multiple_of(jnp.where(j < nlow, j * tn, j * tn + 2 * H), 2 * H)
    nxt = lambda i: jnp.minimum(i + 1, n - 1)
    return pl.pallas_call(
        functools.partial(_inproj_kernel, slab=slab, nj=nj),
        grid=(n, nj),
        in_specs=[
            pl.BlockSpec((tm, D), lambda i, j: (0, 0)),
            pl.BlockSpec((tm, D), lambda i, j: (nxt(i), 0)),
            _layer_spec((1, D), l),
            _mod_spec(l, 0, lambda i, j: row_fn(0)),
            _mod_spec(l, 1, lambda i, j: row_fn(0)),
            _mod_spec(l, 0, lambda i, j: row_fn(nxt(i))),
            _mod_spec(l, 1, lambda i, j: row_fn(nxt(i))),
            pl.BlockSpec((pl.Element(1), pl.Element(tn), pl.Element(D)), lambda i, j: (l, w_row(j), 0)),
            _layer_spec((DTW, D), l),
            _layer_spec((1, DTW), l),
        ],
        out_specs=[
            pl.BlockSpec((tm, tn), lambda i, j: (i, j)),
            pl.BlockSpec((tm, DTW), lambda i, j: (i, 0)),
            pl.BlockSpec((DTW, tm), lambda i, j: (0, i)),
        ],
        out_shape=[
            jax.ShapeDtypeStruct((T, MAIN_W), BF),
            jax.ShapeDtypeStruct((T, DTW), F32),
            jax.ShapeDtypeStruct((DTW, T), F32),
        ],
        scratch_shapes=[pltpu.VMEM((2, tm, D), BF), pltpu.VMEM((2, tm, DTW), F32)],
        compiler_params=_params(("arbitrary", "arbitrary")),
        name="in_proj",
    )(x2, x2, g, mod3, mod3, mod3, mod3, w_main, w_dt, dt_bias)


CH = 64
SUBL = 8
CONV_SUB = 256


def _shift_table():
    s = np.zeros((Q // SUBL, KC - 1, SUBL, 2 * Q), np.float32)
    t = np.arange(Q)
    for si, k in enumerate([k for k in range(KC) if k != KC // 2]):
        s[t // SUBL, si, t % SUBL, CH + t + k - KC // 2] = 1.0
    return s.reshape((KC - 1) * Q, 2 * Q)


def _conv_kernel(prev_ref, cur_ref, next_ref, s_ref, w_ref, b_ref, o_ref, ext_ref, *, tl, nl, transpose):
    i = pl.program_id(1)
    prev = prev_ref[...]
    nxt = next_ref[...]
    ext_ref[0:CH, :] = jnp.where(i > 0, prev, jnp.zeros_like(prev))
    ext_ref[CH:CH + tl, :] = cur_ref[...]
    ext_ref[CH + tl:2 * CH + tl, :] = jnp.where(i < nl - 1, nxt, jnp.zeros_like(nxt))
    tc = cur_ref.shape[1]
    for r in range(tl // Q):
        for c0 in range(0, tc, CONV_SUB):
            cols = slice(c0, c0 + CONV_SUB)
            win = ext_ref[r * Q:(r + 2) * Q, cols]
            sh = _dot(s_ref[...], win)
            acc = b_ref[:, cols]
            si = 0
            for k in range(KC):
                if k == KC // 2:
                    tap = win[CH:CH + Q, :].astype(F32)
                else:
                    starts = [(tg * (KC - 1) + si) * SUBL for tg in range(Q // SUBL)]
                    tap = jnp.concatenate([sh[a:a + SUBL, :] for a in starts], axis=0)
                    si += 1
                acc = acc + w_ref[k:k + 1, cols] * tap
            out = _silu(acc)
            if transpose:
                o_ref[cols, r * Q:(r + 1) * Q] = out.T.astype(BF)
            else:
                o_ref[r * Q:(r + 1) * Q, cols] = out.astype(BF)


def _conv(main, shift, conv_w, conv_b, l, nb, L, transpose):
    T = nb * L
    tc = 1024
    tl = min(L, 1024)
    nl = L // tl
    rb = tl // CH
    last = T // CH - 1
    if transpose:
        ncol = GN // tc
        in_cb = lambda j: j + C_B // tc
        w_cb = lambda j: j + (C_B - C_X) // tc
        out_spec = pl.BlockSpec((tc, tl), lambda b, i, j: (j, b * nl + i))
        out_shape = jax.ShapeDtypeStruct((GN, T), BF)
    else:
        nx = DI // tc
        ncol = (DI + GN) // tc
        in_cb = lambda j: jnp.where(j < nx, j + C_X // tc, j - nx + C_C // tc)
        w_cb = lambda j: jnp.where(j < nx, j, j - nx + (C_C - C_X) // tc)
        out_spec = pl.BlockSpec((tl, tc), lambda b, i, j: (b * nl + i, j))
        out_shape = jax.ShapeDtypeStruct((T, DI + GN), BF)
    return pl.pallas_call(
        functools.partial(_conv_kernel, tl=tl, nl=nl, transpose=transpose),
        grid=(nb, nl, ncol),
        in_specs=[
            pl.BlockSpec((CH, tc), lambda b, i, j: (jnp.maximum((b * nl + i) * rb - 1, 0), in_cb(j))),
            pl.BlockSpec((tl, tc), lambda b, i, j: (b * nl + i, in_cb(j))),
            pl.BlockSpec((CH, tc), lambda b, i, j: (jnp.minimum((b * nl + i + 1) * rb, last), in_cb(j))),
            pl.BlockSpec(((KC - 1) * Q, 2 * Q), lambda b, i, j: (0, 0)),
            pl.BlockSpec((None, KC, tc), lambda b, i, j: (l, 0, w_cb(j))),
            pl.BlockSpec((None, 1, tc), lambda b, i, j: (l, 0, w_cb(j))),
        ],
        out_specs=out_spec,
        out_shape=out_shape,
        scratch_shapes=[pltpu.VMEM((tl + 2 * CH, tc), BF)],
        compiler_params=_params(("arbitrary", "arbitrary", "arbitrary")),
        name="conv_t" if transpose else "conv",
    )(main, main, main, shift, conv_w, conv_b)


def _tri(kind):
    r = lax.broadcasted_iota(jnp.int32, (Q, Q), 0)
    c = lax.broadcasted_iota(jnp.int32, (Q, Q), 1)
    return (c <= r) if kind == "le" else (c >= r)


def _expand_table():
    e = np.zeros((DTW, 2 * DI), np.float32)
    for k in range(2 * H):
        e[k, k * P:(k + 1) * P] = 1.0
    return np.concatenate([e, e], axis=0)


XROWS = 16


def _expand_lhs(w, dec):
    w_hi, w_lo = _split2(w)
    d_hi, d_mid, d_lo = [t.astype(F32) for t in _split3(jnp.broadcast_to(dec, (XROWS, DTW)))]
    rid = lax.broadcasted_iota(jnp.int32, (XROWS, DTW), 0)
    extra = jnp.where(rid == 0, d_hi, jnp.where(rid == 1, d_mid, jnp.where(rid == 2, d_lo, 0.0))).astype(BF)
    zero = jnp.zeros((XROWS, DTW), BF)
    return jnp.concatenate([jnp.concatenate([w_hi, w_lo], axis=1), jnp.concatenate([extra, zero], axis=1)], axis=0)


MAX_CPS = 4


def _ssd_state_kernel(*refs, nc, has_init):
    CPS = refs[-1].shape[0]
    if has_init:
        (xf_ref, btf_ref, dtf_ref, xb_ref, btb_ref, dtb_ref, init_ref, alr_ref, e2_ref,
         entf_ref, entb_ref, fin_ref, st_ref, xw_ref, dec_ref) = refs
    else:
        (xf_ref, btf_ref, dtf_ref, xb_ref, btb_ref, dtb_ref, alr_ref, e2_ref,
         entf_ref, entb_ref, fin_ref, st_ref, xw_ref, dec_ref) = refs
    c = pl.program_id(1)

    @pl.when(c == 0)
    def _():
        st_ref[...] = init_ref[...] if has_init else jnp.zeros_like(st_ref)

    a_row = -jnp.exp(alr_ref[...])
    dirs = ((xf_ref, btf_ref, dtf_ref, entf_ref), (xb_ref, btb_ref, dtb_ref, entb_ref))
    order = [[(s, s) for s in range(CPS)], [(s, CPS - 1 - s) for s in range(CPS)]]
    for d, (x_ref, bt_ref, dt_ref, ent_ref) in enumerate(dirs):
        for s, ck in order[d]:
            rows = slice(ck * Q, (ck + 1) * Q)
            dt = dt_ref[rows, :]
            acum = _dot3_rhs(_tri("le" if d == 0 else "ge").astype(BF), dt * a_row)
            tot = acum[Q - 1:Q] if d == 0 else acum[0:1]
            ex = _dot(_expand_lhs(dt * jnp.exp(tot - acum), jnp.exp(tot)), e2_ref[:, d * DI:(d + 1) * DI])
            dec_ref[s, d] = (ex[Q + 2:Q + 3] + ex[Q + 1:Q + 2]) + ex[Q:Q + 1]
            xw_ref[s, d] = (x_ref[rows, :].astype(F32) * ex[0:Q]).astype(BF)
    for s in range(CPS):
        for d, (x_ref, bt_ref, dt_ref, ent_ref) in enumerate(dirs):
            ck = order[d][s][1]
            for g in range(G):
                cols = slice(g * GW, (g + 1) * GW)
                upd = _dot(bt_ref[g * N:(g + 1) * N, ck * Q:(ck + 1) * Q], xw_ref[s, d, :, cols])
                st = st_ref[d, g]
                ent_ref[ck, g] = st.astype(BF)
                st_ref[d, g] = st * dec_ref[s, d, :, cols] + upd

    @pl.when(c == nc - 1)
    def _():
        fin_ref[...] = st_ref[...]


def _ssd_state(xc, bt, dt, init, alog_r, expand2, l, nb, L):
    CPS = min(MAX_CPS, L // Q)
    nc = L // (CPS * Q)
    fw = lambda b, c: b * nc + c
    bw = lambda b, c: b * nc + (nc - 1 - c)
    has_init = init is not None
    chunk_specs = lambda ch: [
        pl.BlockSpec((CPS * Q, DI), lambda b, c: (ch(b, c), 0)),
        pl.BlockSpec((GN, CPS * Q), lambda b, c: (0, ch(b, c))),
        pl.BlockSpec((CPS * Q, DTW), lambda b, c: (ch(b, c), 0)),
    ]
    st_spec = pl.BlockSpec((None, 2, G, N, GW), lambda b, c: (b, 0, 0, 0, 0))
    in_specs = chunk_specs(fw) + chunk_specs(bw) + ([st_spec] if has_init else []) + [
        _layer_spec((1, DTW), l),
        pl.BlockSpec((2 * DTW, 2 * DI), lambda b, c: (0, 0)),
    ]
    args = (xc, bt, dt, xc, bt, dt) + ((init,) if has_init else ()) + (alog_r, expand2)
    return pl.pallas_call(
        functools.partial(_ssd_state_kernel, nc=nc, has_init=has_init),
        grid=(nb, nc),
        in_specs=in_specs,
        out_specs=[
            pl.BlockSpec((None, CPS, G, N, GW), lambda b, c: (b, c, 0, 0, 0)),
            pl.BlockSpec((None, CPS, G, N, GW), lambda b, c: (b, nc - 1 - c, 0, 0, 0)),
            st_spec,
        ],
        out_shape=[
            jax.ShapeDtypeStruct((nb, nc * CPS, G, N, GW), BF),
            jax.ShapeDtypeStruct((nb, nc * CPS, G, N, GW), BF),
            jax.ShapeDtypeStruct((nb, 2, G, N, GW), F32),
        ],
        scratch_shapes=[pltpu.VMEM((2, G, N, GW), F32), pltpu.VMEM((CPS, 2, Q, DI), BF),
                        pltpu.VMEM((CPS, 2, 1, DI), F32)],
        compiler_params=_params(("arbitrary", "arbitrary")),
        name="ssd_state",
    )(*args)


OUT_CPS = 4


def _fill_blockdiag(bd_ref, xg):
    blk = lax.broadcasted_iota(jnp.int32, (Q, GW), 1) // P
    for r in range(R):
        bd_ref[r * Q:(r + 1) * Q, :] = jnp.where(blk == r, xg, jnp.zeros_like(xg))


def _ssd_out_kernel(x_ref, c_ref, bt_ref, dt_ref, dtt_ref, entf_ref, entb_ref,
                    alr_ref, alc_ref, dsk_ref, o_ref, bd_ref):
    a_row = -jnp.exp(alr_ref[...])
    a_col = -jnp.exp(alc_ref[...])
    neg_inf = jnp.float32(-jnp.inf)
    left = lax.broadcasted_iota(jnp.int32, (Q, 2 * P), 1) < P
    mask = [_tri("le"), _tri("ge")]
    ents = (entf_ref, entb_ref)
    dsum = dsk_ref[0:1, :] + dsk_ref[1:2, :]
    for s in range(o_ref.shape[0] // Q):
        rows = slice(s * Q, (s + 1) * Q)
        dt = dt_ref[rows, :]
        dtt = dtt_ref[:, rows]
        log_dtt = jnp.log(dtt)
        acum, rowv = [], []
        for d in range(2):
            acum.append(_dot3_rhs(mask[d].astype(BF), dt * a_row) * LOG2E)
            acum_t = _dot3_lhs(dtt * a_col, mask[1 - d].astype(BF))
            rowv.append((log_dtt - acum_t) * LOG2E)
        for g in range(G):
            cols = slice(g * GW, (g + 1) * GW)
            xg = x_ref[rows, cols]
            cg = c_ref[rows, g * N:(g + 1) * N]
            cb = _dot(cg, bt_ref[g * N:(g + 1) * N, rows])
            _fill_blockdiag(bd_ref.at[s, g], xg)
            y = dsum[:, cols] * xg.astype(F32)
            for d in range(2):
                ms, e_x = [], []
                for r in range(0, R, 2):
                    es = []
                    for k in (d * H + g * R + r, d * H + g * R + r + 1):
                        acol = jnp.broadcast_to(acum[d][:, k:k + 1], (Q, Q))
                        expo = acol + rowv[d][k:k + 1, :]
                        ms.append((cb * jnp.exp2(jnp.where(mask[d], expo, neg_inf))).astype(BF))
                        es.append(jnp.exp2(acol))
                    e_x.append(jnp.where(left, es[0], es[1]))
                y = y + _dot(jnp.concatenate(ms, axis=1), bd_ref[s, g])
                y = y + _dot(cg, ents[d][s, g]) * jnp.concatenate(e_x, axis=1)
            o_ref[rows, cols] = y.astype(BF)


def _ssd_out(xc, bt, dt, dtt, entf, entb, alog_r, alog_c, dskip_x, l, nb, L):
    cps = min(OUT_CPS, L // Q)
    rows = cps * Q
    nc = L // rows
    T = nb * L
    ch = lambda b, c: b * nc + c
    ent_spec = pl.BlockSpec((None, cps, G, N, GW), lambda b, c: (b, c, 0, 0, 0))
    return pl.pallas_call(
        _ssd_out_kernel,
        grid=(nb, nc),
        in_specs=[
            pl.BlockSpec((rows, DI), lambda b, c: (ch(b, c), 0)),
            pl.BlockSpec((rows, GN), lambda b, c: (ch(b, c), DI // GN)),
            pl.BlockSpec((GN, rows), lambda b, c: (0, ch(b, c))),
            pl.BlockSpec((rows, DTW), lambda b, c: (ch(b, c), 0)),
            pl.BlockSpec((DTW, rows), lambda b, c: (0, ch(b, c))),
            ent_spec,
            ent_spec,
            _layer_spec((1, DTW), l),
            _layer_spec((DTW, 1), l),
            _layer_spec((2, DI), l),
        ],
        out_specs=pl.BlockSpec((rows, DI), lambda b, c: (ch(b, c), 0)),
        out_shape=jax.ShapeDtypeStruct((T, DI), BF),
        scratch_shapes=[pltpu.VMEM((cps, G, R * Q, GW), BF)],
        compiler_params=_params(("parallel", "parallel")),
        name="ssd_out",
    )(xc, xc, bt, dt, dtt, entf, entb, alog_r, alog_c, dskip_x)


TM_MIX = 256
MIX_SUB = 4


def _mixer_kernel(y_ref, z_ref, u_ref, gs_ref, gp_ref, x_ref, ga_ref, nw_ref, band_ref, cnt_ref, wso_ref,
                  pw_ref, psc_ref, wpo_ref, wo_ref, o_ref):
    for s in range(MIX_SUB):
        rows = slice(s * TM_MIX, (s + 1) * TM_MIX)
        yns = []
        for g in range(G):
            cols = slice(g * GW, (g + 1) * GW)
            yz = y_ref[rows, cols].astype(F32) * z_ref[rows, cols].astype(F32)
            yn = yz * lax.rsqrt(jnp.mean(yz * yz, axis=-1, keepdims=True) + EPS)
            yns.append((yn * nw_ref[:, cols]).astype(BF))
        o_ssd = _dot(jnp.concatenate(yns, axis=1), wso_ref[...])
        pms = []
        for gi in range(NPW):
            ug = u_ref[rows, gi * PG:(gi + 1) * PG]
            wsum = _dot(band_ref[gi], ug)
            pm = wsum / cnt_ref[:, gi:gi + 1] - ug.astype(F32)
            pms.append(_dot(pm.astype(BF), pw_ref[gi]))
        pmc = (jnp.concatenate(pms, axis=1) * psc_ref[...]).astype(BF)
        o_pool = _dot(pmc, wpo_ref[...])
        mix = gs_ref[rows, :].astype(F32) * o_ssd + gp_ref[rows, :].astype(F32) * o_pool
        o_ref[rows, :] = x_ref[rows, :] + ga_ref[...] * _dot(mix.astype(BF), wo_ref[...])


def _pool_tables(seg):
    t = np.arange(TM_MIX)
    same = (t[:, None] // seg) == (t[None, :] // seg)
    ts = t % seg
    bands, cnts = [], []
    for k in POOL_WINDOWS:
        lo = t[:, None] - k // 2
        hi = t[:, None] + k // 2
        bands.append(same & (t[None, :] >= lo) & (t[None, :] < hi))
        cnts.append(np.minimum(ts + k // 2, seg) - np.maximum(ts - k // 2, 0))
    cnt = np.ones((TM_MIX, 128), np.float32)
    cnt[:, :NPW] = np.stack(cnts, axis=1)
    return jnp.asarray(np.stack(bands).astype(np.float32), BF), jnp.asarray(cnt)


def _mixer_out(y, main, x2, mod3, l, row_fn, seg, norm_w, w_ssd_out, pool_w, pool_scale, w_pool_out, w_out):
    T = x2.shape[0]
    tm = TM_MIX * MIX_SUB
    band, cnt = _pool_tables(seg)
    return pl.pallas_call(
        _mixer_kernel,
        grid=(T // tm,),
        in_specs=[
            pl.BlockSpec((tm, DI), lambda i: (i, 0)),
            pl.BlockSpec((tm, DI), lambda i: (i, C_Z // DI)),
            pl.BlockSpec((tm, D), lambda i: (i, C_POOL // D)),
            pl.BlockSpec((tm, D), lambda i: (i, C_GS // D)),
            pl.BlockSpec((tm, D), lambda i: (i, C_GP // D)),
            pl.BlockSpec((tm, D), lambda i: (i, 0)),
            _mod_spec(l, 2, row_fn),
            _layer_spec((1, DI), l),
            pl.BlockSpec((NPW, TM_MIX, TM_MIX), lambda i: (0, 0, 0)),
            pl.BlockSpec((TM_MIX, 128), lambda i: (0, 0)),
            _layer_spec((DI, D), l),
            _layer_spec((NPW, PG, PG), l),
            _layer_spec((1, D), l),
            _layer_spec((D, D), l),
            _layer_spec((D, D), l),
        ],
        out_specs=pl.BlockSpec((tm, D), lambda i: (i, 0)),
        out_shape=jax.ShapeDtypeStruct((T, D), F32),
        compiler_params=_params(("parallel",)),
        name="mixer_out",
    )(y, main, main, main, main, x2, mod3, norm_w, band, cnt, w_ssd_out, pool_w, pool_scale, w_pool_out, w_out)


FFN_CHUNKS = (768, 768, 768, 512)


def _ffn_kernel(x_ref, xn_ref, g_ref, sh_ref, sc_ref, ga_ref, shn_ref, scn_ref, wg_ref, wu_ref, wd_ref, gf_ref,
                o_ref, h_ref, *, final):
    i = pl.program_id(0)
    slot = i % 2

    def norm_mod(x, sh, sc):
        y = x * lax.rsqrt(jnp.mean(x * x, axis=-1, keepdims=True) + EPS)
        return ((y * g_ref[...]) * (1.0 + sc) + sh).astype(BF)

    @pl.when(i == 0)
    def _():
        h_ref[0] = norm_mod(x_ref[...], sh_ref[...], sc_ref[...])

    h_ref[1 - slot] = norm_mod(xn_ref[...], shn_ref[...], scn_ref[...])
    h = h_ref[slot]
    acc = None
    c0 = 0
    for cw in FFN_CHUNKS:
        act = (_silu(_dot(h, wg_ref[:, c0:c0 + cw])) * _dot(h, wu_ref[:, c0:c0 + cw])).astype(BF)
        part = _dot(act, wd_ref[c0:c0 + cw, :])
        acc = part if acc is None else acc + part
        c0 += cw
    xn = x_ref[...] + ga_ref[...] * acc
    if final:
        xn = (xn * lax.rsqrt(jnp.mean(xn * xn, axis=-1, keepdims=True) + EPS)) * gf_ref[...]
    o_ref[...] = xn


def _ffn(x2, g, mod3, l, row_fn, w_gate_up, w_down, g_final, tm, final):
    assert sum(FFN_CHUNKS) == FH
    T = x2.shape[0]
    n = T // tm
    nxt = lambda i: jnp.minimum(i + 1, n - 1)
    once = pl.Buffered(1)
    return pl.pallas_call(
        functools.partial(_ffn_kernel, final=final),
        grid=(n,),
        in_specs=[
            pl.BlockSpec((tm, D), lambda i: (i, 0)),
            pl.BlockSpec((tm, D), lambda i: (nxt(i), 0)),
            _layer_spec((1, D), l),
            _mod_spec(l, 3, row_fn),
            _mod_spec(l, 4, row_fn),
            _mod_spec(l, 5, row_fn),
            _mod_spec(l, 3, lambda i: row_fn(nxt(i))),
            _mod_spec(l, 4, lambda i: row_fn(nxt(i))),
            pl.BlockSpec((None, D, FH), lambda i: (l, 0, 0), pipeline_mode=once),
            pl.BlockSpec((None, D, FH), lambda i: (l, 0, 1), pipeline_mode=once),
            pl.BlockSpec((None, FH, D), lambda i: (l, 0, 0), pipeline_mode=once),
            pl.BlockSpec((1, D), lambda i: (0, 0)),
        ],
        out_specs=pl.BlockSpec((tm, D), lambda i: (i, 0)),
        out_shape=jax.ShapeDtypeStruct((T, D), F32),
        scratch_shapes=[pltpu.VMEM((2, tm, D), BF)],
        compiler_params=_params(("arbitrary",)),
        name="ffn",
    )(x2, x2, g, mod3, mod3, mod3, mod3, mod3, w_gate_up, w_gate_up, w_down, g_final)


def kernel(x, c, ctx, c_ctx, w_ada, b_ada, g_mix, w_in, conv_w, conv_b, dt_bias, a_log, d_skip, ssd_norm_w,
           w_ssd_out, pool_w, pool_scale, w_pool_out, w_out, g_ffn, w_gate_up, w_down, g_final):
    nb, L, _ = x.shape
    Lc = ctx.shape[1]
    ctx_row = nb

    cond8 = jnp.zeros((8, D), F32).at[:nb].set(c).at[ctx_row].set(c_ctx)
    mod3 = _adaln(cond8, w_ada, b_ada).reshape(DEPTH * 8 * 6, 1, D)

    shift = jnp.asarray(_shift_table(), BF)
    expand2 = jnp.asarray(_expand_table(), BF)

    w_in_t = jnp.swapaxes(w_in, 1, 2)
    w_main = w_in_t.astype(BF)
    w_dt = jnp.pad(w_in_t[:, C_POOL:C_POOL + 2 * H], ((0, 0), (0, DTW - 2 * H), (0, 0)))
    pad_heads = lambda a: jnp.pad(a.reshape(DEPTH, 1, 2 * H), ((0, 0), (0, 0), (0, DTW - 2 * H)))
    dtb = pad_heads(dt_bias)
    alog_r = pad_heads(a_log)
    alog_c = alog_r.reshape(DEPTH, DTW, 1)
    dskip_x = jnp.repeat(d_skip, P, axis=2)
    norm_w = ssd_norm_w.reshape(DEPTH, 1, DI)
    gm = g_mix.reshape(DEPTH, 1, D)
    gf = g_ffn.reshape(DEPTH, 1, D)
    cbias = conv_b.reshape(DEPTH, 1, -1)
    wso = w_ssd_out.astype(BF)
    pw = pool_w.astype(BF)
    psc = pool_scale.reshape(DEPTH, 1, D)
    wpo = w_pool_out.astype(BF)
    wo = w_out.astype(BF)
    wgu = w_gate_up.astype(BF)
    wdn = w_down.astype(BF)
    gfin = g_final.reshape(1, D)

    xl = x.reshape(nb * L, D)
    xc_ = ctx.reshape(nb * Lc, D)
    tm_l = 1024
    tm_f = 512
    for l in range(DEPTH):
        last = l == DEPTH - 1

        def prepare(x2, row_fn, tm, seq):
            main, dt, dtt = _in_proj(x2, gm, mod3, l, row_fn, w_main, w_dt, dtb, tm)
            xcv = _conv(main, shift, conv_w, cbias, l, nb, seq, transpose=False)
            bt = _conv(main, shift, conv_w, cbias, l, nb, seq, transpose=True)
            return main, dt, dtt, xcv, bt

        def mix_and_ffn(x2, parts, ent, row_fn_mix, row_fn_ffn, seq, seg, final):
            main, dt, dtt, xcv, bt = parts
            y = _ssd_out(xcv, bt, dt, dtt, ent[0], ent[1], alog_r, alog_c, dskip_x, l, nb, seq)
            x2 = _mixer_out(y, main, x2, mod3, l, row_fn_mix, seg, norm_w, wso, pw, psc, wpo, wo)
            return _ffn(x2, gf, mod3, l, row_fn_ffn, wgu, wdn, gfin, tm_f, final)

        cparts = prepare(xc_, lambda i: ctx_row, 1024, Lc)
        centf, centb, cfin = _ssd_state(cparts[3], cparts[4], cparts[1], None, alog_r, expand2, l, nb, Lc)
        lparts = prepare(xl, lambda i: i // (L // tm_l), tm_l, L)
        lentf, lentb, _ = _ssd_state(lparts[3], lparts[4], lparts[1], cfin, alog_r, expand2, l, nb, L)
        xl = mix_and_ffn(xl, lparts, (lentf, lentb), lambda i: i // (L // (TM_MIX * MIX_SUB)),
                         lambda i: i // (L // tm_f), L, GRID_W, last)
        if not last:
            xc_ = mix_and_ffn(xc_, cparts, (centf, centb), lambda i: ctx_row, lambda i: ctx_row, Lc, Lc, False)
    return xl.reshape(nb, L, D)
```

```python
import functools

import numpy as np
import jax
import jax.numpy as jnp
from jax import lax
from jax.experimental import pallas as pl
from jax.experimental.pallas import tpu as pltpu

F32 = jnp.float32
BF = jnp.bfloat16

D = 1024
DEPTH = 2
H = 32
P = 64
G = 8
R = H // G
N = 128
Q = 128
DI = H * P
GN = G * N
GW = R * P
KC = 5
FH = 2816
POOL_WINDOWS = (2, 4, 8, 16)
NPW = len(POOL_WINDOWS)
PG = D // NPW
GRID_W = 64
EPS = 1e-6
LOG2E = 1.4426950408889634
DTW = 128

C_Z, C_X, C_B, C_C, C_POOL, C_GS, C_GP = 0, 2048, 4096, 5120, 6144, 7168, 8192
MAIN_W = 9216

VMEM_LIMIT = 56 * 1024 * 1024


def _dot(a, b):
    return jnp.dot(a, b, preferred_element_type=F32)


def _dot_nt(a, b):
    return lax.dot_general(a, b, (((1,), (1,)), ((), ())), preferred_element_type=F32)


def _split2(a):
    hi = a.astype(BF)
    return hi, (a - hi.astype(F32)).astype(BF)


def _split3(a):
    hi = a.astype(BF)
    r = a - hi.astype(F32)
    mid = r.astype(BF)
    lo = (r - mid.astype(F32)).astype(BF)
    return hi, mid, lo


def _dot3_rhs(m, a):
    hi, mid, lo = _split3(a)
    return (_dot(m, lo) + _dot(m, mid)) + _dot(m, hi)


def _dot3_lhs(a, m):
    hi, mid, lo = _split3(a)
    return (_dot(lo, m) + _dot(mid, m)) + _dot(hi, m)


def _silu(v):
    return v * jax.nn.sigmoid(v)


def _params(sem):
    return pltpu.CompilerParams(dimension_semantics=sem, vmem_limit_bytes=VMEM_LIMIT)


def _mod_spec(l, k, row_fn):
    return pl.BlockSpec((None, 1, D), lambda *ids: ((l * 8 + row_fn(*ids)) * 6 + k, 0, 0))


def _layer_spec(shape, l):
    zeros = (0,) * len(shape)
    return pl.BlockSpec((None,) + tuple(shape), lambda *ids: (l,) + zeros)


def _adaln_kernel(c_ref, w_ref, b_ref, o_ref):
    s_hi, s_lo = _split2(_silu(c_ref[...]))
    w_hi, w_lo = _split2(w_ref[...])
    acc = (_dot(s_lo, w_hi) + _dot(s_hi, w_lo)) + _dot(s_hi, w_hi)
    o_ref[...] = acc + b_ref[...]


def _adaln(cond8, w_ada, b_ada):
    tn = 1536
    return pl.pallas_call(
        _adaln_kernel,
        grid=(DEPTH, 6 * D // tn),
        in_specs=[
            pl.BlockSpec((8, D), lambda l, j: (0, 0)),
            pl.BlockSpec((None, D, tn), lambda l, j: (l, 0, j)),
            pl.BlockSpec((None, 1, tn), lambda l, j: (l, 0, j)),
        ],
        out_specs=pl.BlockSpec((None, 8, tn), lambda l, j: (l, 0, j)),
        out_shape=jax.ShapeDtypeStruct((DEPTH, 8, 6 * D), F32),
        compiler_params=_params(("arbitrary", "arbitrary")),
        name="adaln",
    )(cond8, w_ada, b_ada.reshape(DEPTH, 1, 6 * D))


def _inproj_kernel(x_ref, xn_ref, g_ref, sh_ref, sc_ref, shn_ref, scn_ref, w_ref, wdt_ref, dtb_ref,
                   o_ref, dt_ref, dtt_ref, h_ref, dts_ref, *, slab, nj):
    i = pl.program_id(0)
    j = pl.program_id(1)
    slot = i % 2
    tm, tn = o_ref.shape
    wh, wl = _split2(wdt_ref[...])
    wcat = jnp.concatenate([wh, wl], axis=0)

    def prologue(x, sh, sc):
        y = x * lax.rsqrt(jnp.mean(x * x, axis=-1, keepdims=True) + EPS)
        h = (y * g_ref[...]) * (1.0 + sc) + sh
        hb, hl = _split2(h)
        p = _dot_nt(hb, wcat)
        d = (_dot_nt(hl, wh) + p[:, DTW:]) + p[:, :DTW] + dtb_ref[...]
        return hb, jnp.maximum(d, 0.0) + jnp.log1p(jnp.exp(-jnp.abs(d)))

    @pl.when((i == 0) & (j == 0))
    def _():
        hb, dt = prologue(x_ref[...], sh_ref[...], sc_ref[...])
        h_ref[0] = hb
        dts_ref[0] = dt

    @pl.when(j == 0)
    def _():
        dt = dts_ref[slot]
        dt_ref[...] = dt
        dtt_ref[...] = dt.T

    def step(jj):
        rows = slice(min(jj * slab, tm - slab), min(jj * slab, tm - slab) + slab)
        hb, dt = prologue(xn_ref[rows, :], shn_ref[...], scn_ref[...])
        h_ref[1 - slot, rows, :] = hb
        dts_ref[1 - slot, rows, :] = dt
        v = _dot_nt(h_ref[slot], w_ref[0])
        lo, hi = jj * tn, (jj + 1) * tn
        n_silu = min(max(C_X - lo, 0), tn)
        n_sig = min(max(hi - C_GS, 0), tn)
        if n_silu:
            o_ref[:, :n_silu] = _silu(v[:, :n_silu]).astype(BF)
        if tn - n_silu - n_sig:
            o_ref[:, n_silu:tn - n_sig] = v[:, n_silu:tn - n_sig].astype(BF)
        if n_sig:
            o_ref[:, tn - n_sig:] = jax.nn.sigmoid(v[:, tn - n_sig:]).astype(BF)

    for jj in range(nj):
        pl.when(j == jj)(functools.partial(step, jj))


def _in_proj(x2, g, mod3, l, row_fn, w_main, w_dt, dt_bias, tm):
    T = x2.shape[0]
    n = T // tm
    tn = 3072
    nj = MAIN_W // tn
    slab = -(-tm // (nj * Q)) * Q
    nlow = C_POOL // tn
    w_row = lambda j: pl.multiple_of(jnp.where(j < nlow, j * tn, j * tn + 2 * H), 2 * H)
    nxt = lambda i: jnp.minimum(i + 1, n - 1)
    return pl.pallas_call(
        functools.partial(_inproj_kernel, slab=slab, nj=nj),
        grid=(n, nj),
        in_specs=[
            pl.BlockSpec((tm, D), lambda i, j: (0, 0)),
            pl.BlockSpec((tm, D), lambda i, j: (nxt(i), 0)),
            _layer_spec((1, D), l),
            _mod_spec(l, 0, lambda i, j: row_fn(0)),
            _mod_spec(l, 1, lambda i, j: row_fn(0)),
            _mod_spec(l, 0, lambda i, j: row_fn(nxt(i))),
            _mod_spec(l, 1, lambda i, j: row_fn(nxt(i))),
            pl.BlockSpec((pl.Element(1), pl.Element(tn), pl.Element(D)), lambda i, j: (l, w_row(j), 0)),
            _layer_spec((DTW, D), l),
            _layer_spec((1, DTW), l),
        ],
        out_specs=[
            pl.BlockSpec((tm, tn), lambda i, j: (i, j)),
            pl.BlockSpec((tm, DTW), lambda i, j: (i, 0)),
            pl.BlockSpec((DTW, tm), lambda i, j: (0, i)),
        ],
        out_shape=[
            jax.ShapeDtypeStruct((T, MAIN_W), BF),
            jax.ShapeDtypeStruct((T, DTW), F32),
            jax.ShapeDtypeStruct((DTW, T), F32),
        ],
        scratch_shapes=[pltpu.VMEM((2, tm, D), BF), pltpu.VMEM((2, tm, DTW), F32)],
        compiler_params=_params(("arbitrary", "arbitrary")),
        name="in_proj",
    )(x2, x2, g, mod3, mod3, mod3, mod3, w_main, w_dt, dt_bias)


CH = 64
SUBL = 8
CONV_SUB = 256


def _shift_table():
    s = np.zeros((Q // SUBL, KC - 1, SUBL, 2 * Q), np.float32)
    t = np.arange(Q)
    for si, k in enumerate([k for k in range(KC) if k != KC // 2]):
        s[t // SUBL, si, t % SUBL, CH + t + k - KC // 2] = 1.0
    return s.reshape((KC - 1) * Q, 2 * Q)


def _conv_kernel(prev_ref, cur_ref, next_ref, s_ref, w_ref, b_ref, o_ref, ext_ref, *, tl, nl, transpose):
    i = pl.program_id(1)
    prev = prev_ref[...]
    nxt = next_ref[...]
    ext_ref[0:CH, :] = jnp.where(i > 0, prev, jnp.zeros_like(prev))
    ext_ref[CH:CH + tl, :] = cur_ref[...]
    ext_ref[CH + tl:2 * CH + tl, :] = jnp.where(i < nl - 1, nxt, jnp.zeros_like(nxt))
    tc = cur_ref.shape[1]
    for r in range(tl // Q):
        for c0 in range(0, tc, CONV_SUB):
            cols = slice(c0, c0 + CONV_SUB)
            win = ext_ref[r * Q:(r + 2) * Q, cols]
            sh = _dot(s_ref[...], win)
            acc = b_ref[:, cols]
            si = 0
            for k in range(KC):
                if k == KC // 2:
                    tap = win[CH:CH + Q, :].astype(F32)
                else:
                    starts = [(tg * (KC - 1) + si) * SUBL for tg in range(Q // SUBL)]
                    tap = jnp.concatenate([sh[a:a + SUBL, :] for a in starts], axis=0)
                    si += 1
                acc = acc + w_ref[k:k + 1, cols] * tap
            out = _silu(acc)
            if transpose:
                o_ref[cols, r * Q:(r + 1) * Q] = out.T.astype(BF)
            else:
                o_ref[r * Q:(r + 1) * Q, cols] = out.astype(BF)


def _conv(main, shift, conv_w, conv_b, l, nb, L, transpose):
    T = nb * L
    tc = 1024
    tl = min(L, 1024)
    nl = L // tl
    rb = tl // CH
    last = T // CH - 1
    if transpose:
        ncol = GN // tc
        in_cb = lambda j: j + C_B // tc
        w_cb = lambda j: j + (C_B - C_X) // tc
        out_spec = pl.BlockSpec((tc, tl), lambda b, i, j: (j, b * nl + i))
        out_shape = jax.ShapeDtypeStruct((GN, T), BF)
    else:
        nx = DI // tc
        ncol = (DI + GN) // tc
        in_cb = lambda j: jnp.where(j < nx, j + C_X // tc, j - nx + C_C // tc)
        w_cb = lambda j: jnp.where(j < nx, j, j - nx + (C_C - C_X) // tc)
        out_spec = pl.BlockSpec((tl, tc), lambda b, i, j: (b * nl + i, j))
        out_shape = jax.ShapeDtypeStruct((T, DI + GN), BF)
    return pl.pallas_call(
        functools.partial(_conv_kernel, tl=tl, nl=nl, transpose=transpose),
        grid=(nb, nl, ncol),
        in_specs=[
            pl.BlockSpec((CH, tc), lambda b, i, j: (jnp.maximum((b * nl + i) * rb - 1, 0), in_cb(j))),
            pl.BlockSpec((tl, tc), lambda b, i, j: (b * nl + i, in_cb(j))),
            pl.BlockSpec((CH, tc), lambda b, i, j: (jnp.minimum((b * nl + i + 1) * rb, last), in_cb(j))),
            pl.BlockSpec(((KC - 1) * Q, 2 * Q), lambda b, i, j: (0, 0)),
            pl.BlockSpec((None, KC, tc), lambda b, i, j: (l, 0, w_cb(j))),
            pl.BlockSpec((None, 1, tc), lambda b, i, j: (l, 0, w_cb(j))),
        ],
        out_specs=out_spec,
        out_shape=out_shape,
        scratch_shapes=[pltpu.VMEM((tl + 2 * CH, tc), BF)],
        compiler_params=_params(("arbitrary", "arbitrary", "arbitrary")),
        name="conv_t" if transpose else "conv",
    )(main, main, main, shift, conv_w, conv_b)


def _tri(kind):
    r = lax.broadcasted_iota(jnp.int32, (Q, Q), 0)
    c = lax.broadcasted_iota(jnp.int32, (Q, Q), 1)
    return (c <= r) if kind == "le" else (c >= r)


def _expand_table():
    e = np.zeros((DTW, 2 * DI), np.float32)
    for k in range(2 * H):
        e[k, k * P:(k + 1) * P] = 1.0
    return np.concatenate([e, e], axis=0)


XROWS = 16


def _expand_lhs(w, dec):
    w_hi, w_lo = _split2(w)
    d_hi, d_mid, d_lo = [t.astype(F32) for t in _split3(jnp.broadcast_to(dec, (XROWS, DTW)))]
    rid = lax.broadcasted_iota(jnp.int32, (XROWS, DTW), 0)
    extra = jnp.where(rid == 0, d_hi, jnp.where(rid == 1, d_mid, jnp.where(rid == 2, d_lo, 0.0))).astype(BF)
    zero = jnp.zeros((XROWS, DTW), BF)
    return jnp.concatenate([jnp.concatenate([w_hi, w_lo], axis=1), jnp.concatenate([extra, zero], axis=1)], axis=0)


MAX_CPS = 4


def _ssd_state_kernel(*refs, nc, has_init):
    CPS = refs[-1].shape[0]
    if has_init:
        (xf_ref, btf_ref, dtf_ref, xb_ref, btb_ref, dtb_ref, init_ref, alr_ref, e2_ref,
         entf_ref, entb_ref, fin_ref, st_ref, xw_ref, dec_ref) = refs
    else:
        (xf_ref, btf_ref, dtf_ref, xb_ref, btb_ref, dtb_ref, alr_ref, e2_ref,
         entf_ref, entb_ref, fin_ref, st_ref, xw_ref, dec_ref) = refs
    c = pl.program_id(1)

    @pl.when(c == 0)
    def _():
        st_ref[...] = init_ref[...] if has_init else jnp.zeros_like(st_ref)

    a_row = -jnp.exp(alr_ref[...])
    dirs = ((xf_ref, btf_ref, dtf_ref, entf_ref), (xb_ref, btb_ref, dtb_ref, entb_ref))
    order = [[(s, s) for s in range(CPS)], [(s, CPS - 1 - s) for s in range(CPS)]]
    for d, (x_ref, bt_ref, dt_ref, ent_ref) in enumerate(dirs):
        for s, ck in order[d]:
            rows = slice(ck * Q, (ck + 1) * Q)
            dt = dt_ref[rows, :]
            acum = _dot3_rhs(_tri("le" if d == 0 else "ge").astype(BF), dt * a_row)
            tot = acum[Q - 1:Q] if d == 0 else acum[0:1]
            ex = _dot(_expand_lhs(dt * jnp.exp(tot - acum), jnp.exp(tot)), e2_ref[:, d * DI:(d + 1) * DI])
            dec_ref[s, d] = (ex[Q + 2:Q + 3] + ex[Q + 1:Q + 2]) + ex[Q:Q + 1]
            xw_ref[s, d] = (x_ref[rows, :].astype(F32) * ex[0:Q]).astype(BF)
    for s in range(CPS):
        for d, (x_ref, bt_ref, dt_ref, ent_ref) in enumerate(dirs):
            ck = order[d][s][1]
            for g in range(G):
                cols = slice(g * GW, (g + 1) * GW)
                upd = _dot(bt_ref[g * N:(g + 1) * N, ck * Q:(ck + 1) * Q], xw_ref[s, d, :, cols])
                st = st_ref[d, g]
                ent_ref[ck, g] = st.astype(BF)
                st_ref[d, g] = st * dec_ref[s, d, :, cols] + upd

    @pl.when(c == nc - 1)
    def _():
        fin_ref[...] = st_ref[...]


def _ssd_state(xc, bt, dt, init, alog_r, expand2, l, nb, L):
    CPS = min(MAX_CPS, L // Q)
    nc = L // (CPS * Q)
    fw = lambda b, c: b * nc + c
    bw = lambda b, c: b * nc + (nc - 1 - c)
    has_init = init is not None
    chunk_specs = lambda ch: [
        pl.BlockSpec((CPS * Q, DI), lambda b, c: (ch(b, c), 0)),
        pl.BlockSpec((GN, CPS * Q), lambda b, c: (0, ch(b, c))),
        pl.BlockSpec((CPS * Q, DTW), lambda b, c: (ch(b, c), 0)),
    ]
    st_spec = pl.BlockSpec((None, 2, G, N, GW), lambda b, c: (b, 0, 0, 0, 0))
    in_specs = chunk_specs(fw) + chunk_specs(bw) + ([st_spec] if has_init else []) + [
        _layer_spec((1, DTW), l),
        pl.BlockSpec((2 * DTW, 2 * DI), lambda b, c: (0, 0)),
    ]
    args = (xc, bt, dt, xc, bt, dt) + ((init,) if has_init else ()) + (alog_r, expand2)
    return pl.pallas_call(
        functools.partial(_ssd_state_kernel, nc=nc, has_init=has_init),
        grid=(nb, nc),
        in_specs=in_specs,
        out_specs=[
            pl.BlockSpec((None, CPS, G, N, GW), lambda b, c: (b, c, 0, 0, 0)),
            pl.BlockSpec((None, CPS, G, N, GW), lambda b, c: (b, nc - 1 - c, 0, 0, 0)),
            st_spec,
        ],
        out_shape=[
            jax.ShapeDtypeStruct((nb, nc * CPS, G, N, GW), BF),
            jax.ShapeDtypeStruct((nb, nc * CPS, G, N, GW), BF),
            jax.ShapeDtypeStruct((nb, 2, G, N, GW), F32),
        ],
        scratch_shapes=[pltpu.VMEM((2, G, N, GW), F32), pltpu.VMEM((CPS, 2, Q, DI), BF),
                        pltpu.VMEM((CPS, 2, 1, DI), F32)],
        compiler_params=_params(("arbitrary", "arbitrary")),
        name="ssd_state",
    )(*args)


OUT_CPS = 4


def _fill_blockdiag(bd_ref, xg):
    blk = lax.broadcasted_iota(jnp.int32, (Q, GW), 1) // P
    for r in range(R):
        bd_ref[r * Q:(r + 1) * Q, :] = jnp.where(blk == r, xg, jnp.zeros_like(xg))


def _ssd_out_kernel(xc_ref, bt_ref, dt_ref, dtt_ref, entf_ref, entb_ref,
                    alr_ref, alc_ref, dsk_ref, o_ref, bd_ref):
    x_ref = xc_ref.at[:, 0:DI]
    c_ref = xc_ref.at[:, DI:DI + GN]
    a_row = -jnp.exp(alr_ref[...])
    a_col = -jnp.exp(alc_ref[...])
    neg_inf = jnp.float32(-jnp.inf)
    left = lax.broadcasted_iota(jnp.int32, (Q, 2 * P), 1) < P
    mask = [_tri("le"), _tri("ge")]
    ents = (entf_ref, entb_ref)
    dsum = dsk_ref[0:1, :] + dsk_ref[1:2, :]
    for s in range(o_ref.shape[0] // Q):
        rows = slice(s * Q, (s + 1) * Q)
        dt = dt_ref[rows, :]
        dtt = dtt_ref[:, rows]
        log_dtt = jnp.log(dtt)
        acum, rowv = [], []
        for d in range(2):
            acum.append(_dot3_rhs(mask[d].astype(BF), dt * a_row) * LOG2E)
            acum_t = _dot3_lhs(dtt * a_col, mask[1 - d].astype(BF))
            rowv.append((log_dtt - acum_t) * LOG2E)
        for g in range(G):
            cols = slice(g * GW, (g + 1) * GW)
            xg = x_ref[rows, cols]
            cg = c_ref[rows, g * N:(g + 1) * N]
            cbb = _dot(cg, bt_ref[g * N:(g + 1) * N, rows]).astype(BF)
            _fill_blockdiag(bd_ref.at[s, g], xg)
            y = dsum[:, cols] * xg.astype(F32)
            for d in range(2):
                ms, e_x = [], []
                for r in range(0, R, 2):
                    es = []
                    for k in (d * H + g * R + r, d * H + g * R + r + 1):
                        acol = jnp.broadcast_to(acum[d][:, k:k + 1], (Q, Q))
                        expo = acol + rowv[d][k:k + 1, :]
                        ms.append(cbb * jnp.exp2(jnp.where(mask[d], expo, neg_inf)).astype(BF))
                        es.append(jnp.exp2(acol))
                    e_x.append(jnp.where(left, es[0], es[1]))
                y = y + _dot(jnp.concatenate(ms, axis=1), bd_ref[s, g])
                y = y + _dot(cg, ents[d][s, g]) * jnp.concatenate(e_x, axis=1)
            o_ref[rows, cols] = y.astype(BF)


def _ssd_out(xc, bt, dt, dtt, entf, entb, alog_r, alog_c, dskip_x, l, nb, L):
    cps = min(OUT_CPS, L // Q)
    rows = cps * Q
    nc = L // rows
    T = nb * L
    ch = lambda b, c: b * nc + c
    ent_spec = pl.BlockSpec((None, cps, G, N, GW), lambda b, c: (b, c, 0, 0, 0))
    return pl.pallas_call(
        _ssd_out_kernel,
        grid=(nb, nc),
        in_specs=[
            pl.BlockSpec((rows, DI + GN), lambda b, c: (ch(b, c), 0)),
            pl.BlockSpec((GN, rows), lambda b, c: (0, ch(b, c))),
            pl.BlockSpec((rows, DTW), lambda b, c: (ch(b, c), 0)),
            pl.BlockSpec((DTW, rows), lambda b, c: (0, ch(b, c))),
            ent_spec,
            ent_spec,
            _layer_spec((1, DTW), l),
            _layer_spec((DTW, 1), l),
            _layer_spec((2, DI), l),
        ],
        out_specs=pl.BlockSpec((rows, DI), lambda b, c: (ch(b, c), 0)),
        out_shape=jax.ShapeDtypeStruct((T, DI), BF),
        scratch_shapes=[pltpu.VMEM((cps, G, R * Q, GW), BF)],
        compiler_params=_params(("parallel", "parallel")),
        name="ssd_out",
    )(xc, bt, dt, dtt, entf, entb, alog_r, alog_c, dskip_x)


TM_MIX = 256
MIX_SUB = 4


def _mixer_kernel(y_ref, z_ref, pgg_ref, x_ref, ga_ref, nw_ref, band_ref, cnt_ref, wso_ref,
                  pw_ref, psc_ref, wpo_ref, wo_ref, o_ref):
    u_ref = pgg_ref.at[:, 0:D]
    gs_ref = pgg_ref.at[:, C_GS - C_POOL:C_GS - C_POOL + D]
    gp_ref = pgg_ref.at[:, C_GP - C_POOL:C_GP - C_POOL + D]
    for s in range(MIX_SUB):
        rows = slice(s * TM_MIX, (s + 1) * TM_MIX)
        yns = []
        for g in range(G):
            cols = slice(g * GW, (g + 1) * GW)
            yz = y_ref[rows, cols].astype(F32) * z_ref[rows, cols].astype(F32)
            yn = yz * lax.rsqrt(jnp.mean(yz * yz, axis=-1, keepdims=True) + EPS)
            yns.append((yn * nw_ref[:, cols]).astype(BF))
        o_ssd = _dot(jnp.concatenate(yns, axis=1), wso_ref[...])
        pms = []
        for gi in range(NPW):
            ug = u_ref[rows, gi * PG:(gi + 1) * PG]
            wsum = _dot(band_ref[gi], ug)
            pm = wsum / cnt_ref[:, gi:gi + 1] - ug.astype(F32)
            pms.append(_dot(pm.astype(BF), pw_ref[gi]))
        pmc = (jnp.concatenate(pms, axis=1) * psc_ref[...]).astype(BF)
        o_pool = _dot(pmc, wpo_ref[...])
        mix = gs_ref[rows, :].astype(F32) * o_ssd + gp_ref[rows, :].astype(F32) * o_pool
        o_ref[rows, :] = x_ref[rows, :] + ga_ref[...] * _dot(mix.astype(BF), wo_ref[...])


def _pool_tables(seg):
    t = np.arange(TM_MIX)
    same = (t[:, None] // seg) == (t[None, :] // seg)
    ts = t % seg
    bands, cnts = [], []
    for k in POOL_WINDOWS:
        lo = t[:, None] - k // 2
        hi = t[:, None] + k // 2
        bands.append(same & (t[None, :] >= lo) & (t[None, :] < hi))
        cnts.append(np.minimum(ts + k // 2, seg) - np.maximum(ts - k // 2, 0))
    cnt = np.ones((TM_MIX, 128), np.float32)
    cnt[:, :NPW] = np.stack(cnts, axis=1)
    return jnp.asarray(np.stack(bands).astype(np.float32), BF), jnp.asarray(cnt)


def _mixer_out(y, main, x2, mod3, l, row_fn, seg, norm_w, w_ssd_out, pool_w, pool_scale, w_pool_out, w_out):
    T = x2.shape[0]
    tm = TM_MIX * MIX_SUB
    band, cnt = _pool_tables(seg)
    return pl.pallas_call(
        _mixer_kernel,
        grid=(T // tm,),
        in_specs=[
            pl.BlockSpec((tm, DI), lambda i: (i, 0)),
            pl.BlockSpec((tm, DI), lambda i: (i, C_Z // DI)),
            pl.BlockSpec((tm, MAIN_W - C_POOL), lambda i: (i, C_POOL // (MAIN_W - C_POOL))),
            pl.BlockSpec((tm, D), lambda i: (i, 0)),
            _mod_spec(l, 2, row_fn),
            _layer_spec((1, DI), l),
            pl.BlockSpec((NPW, TM_MIX, TM_MIX), lambda i: (0, 0, 0)),
            pl.BlockSpec((TM_MIX, 128), lambda i: (0, 0)),
            _layer_spec((DI, D), l),
            _layer_spec((NPW, PG, PG), l),
            _layer_spec((1, D), l),
            _layer_spec((D, D), l),
            _layer_spec((D, D), l),
        ],
        out_specs=pl.BlockSpec((tm, D), lambda i: (i, 0)),
        out_shape=jax.ShapeDtypeStruct((T, D), F32),
        compiler_params=_params(("parallel",)),
        name="mixer_out",
    )(y, main, main, x2, mod3, norm_w, band, cnt, w_ssd_out, pool_w, pool_scale, w_pool_out, w_out)


FFN_CHUNKS = (768, 768, 768, 512)


def _ffn_kernel(x_ref, xn_ref, g_ref, sh_ref, sc_ref, ga_ref, shn_ref, scn_ref, wg_ref, wu_ref, wd_ref, gf_ref,
                o_ref, h_ref, *, final):
    i = pl.program_id(0)
    slot = i % 2

    def norm_mod(x, sh, sc):
        y = x * lax.rsqrt(jnp.mean(x * x, axis=-1, keepdims=True) + EPS)
        return ((y * g_ref[...]) * (1.0 + sc) + sh).astype(BF)

    @pl.when(i == 0)
    def _():
        h_ref[0] = norm_mod(x_ref[...], sh_ref[...], sc_ref[...])

    h_ref[1 - slot] = norm_mod(xn_ref[...], shn_ref[...], scn_ref[...])
    h = h_ref[slot]
    acc = None
    c0 = 0
    for cw in FFN_CHUNKS:
        act = (_silu(_dot(h, wg_ref[:, c0:c0 + cw])) * _dot(h, wu_ref[:, c0:c0 + cw])).astype(BF)
        part = _dot(act, wd_ref[c0:c0 + cw, :])
        acc = part if acc is None else acc + part
        c0 += cw
    xn = x_ref[...] + ga_ref[...] * acc
    if final:
        xn = (xn * lax.rsqrt(jnp.mean(xn * xn, axis=-1, keepdims=True) + EPS)) * gf_ref[...]
    o_ref[...] = xn


def _ffn(x2, g, mod3, l, row_fn, w_gate_up, w_down, g_final, tm, final):
    assert sum(FFN_CHUNKS) == FH
    T = x2.shape[0]
    n = T // tm
    nxt = lambda i: jnp.minimum(i + 1, n - 1)
    once = pl.Buffered(1)
    return pl.pallas_call(
        functools.partial(_ffn_kernel, final=final),
        grid=(n,),
        in_specs=[
            pl.BlockSpec((tm, D), lambda i: (i, 0)),
            pl.BlockSpec((tm, D), lambda i: (nxt(i), 0)),
            _layer_spec((1, D), l),
            _mod_spec(l, 3, row_fn),
            _mod_spec(l, 4, row_fn),
            _mod_spec(l, 5, row_fn),
            _mod_spec(l, 3, lambda i: row_fn(nxt(i))),
            _mod_spec(l, 4, lambda i: row_fn(nxt(i))),
            pl.BlockSpec((None, D, FH), lambda i: (l, 0, 0), pipeline_mode=once),
            pl.BlockSpec((None, D, FH), lambda i: (l, 0, 1), pipeline_mode=once),
            pl.BlockSpec((None, FH, D), lambda i: (l, 0, 0), pipeline_mode=once),
            pl.BlockSpec((1, D), lambda i: (0, 0)),
        ],
        out_specs=pl.BlockSpec((tm, D), lambda i: (i, 0)),
        out_shape=jax.ShapeDtypeStruct((T, D), F32),
        scratch_shapes=[pltpu.VMEM((2, tm, D), BF)],
        compiler_params=_params(("arbitrary",)),
        name="ffn",
    )(x2, x2, g, mod3, mod3, mod3, mod3, mod3, w_gate_up, w_gate_up, w_down, g_final)


def kernel(x, c, ctx, c_ctx, w_ada, b_ada, g_mix, w_in, conv_w, conv_b, dt_bias, a_log, d_skip, ssd_norm_w,
           w_ssd_out, pool_w, pool_scale, w_pool_out, w_out, g_ffn, w_gate_up, w_down, g_final):
    nb, L, _ = x.shape
    Lc = ctx.shape[1]
    ctx_row = nb

    cond8 = jnp.zeros((8, D), F32).at[:nb].set(c).at[ctx_row].set(c_ctx)
    mod3 = _adaln(cond8, w_ada, b_ada).reshape(DEPTH * 8 * 6, 1, D)

    shift = jnp.asarray(_shift_table(), BF)
    expand2 = jnp.asarray(_expand_table(), BF)

    w_in_t = jnp.swapaxes(w_in, 1, 2)
    w_main = w_in_t.astype(BF)
    w_dt = jnp.pad(w_in_t[:, C_POOL:C_POOL + 2 * H], ((0, 0), (0, DTW - 2 * H), (0, 0)))
    pad_heads = lambda a: jnp.pad(a.reshape(DEPTH, 1, 2 * H), ((0, 0), (0, 0), (0, DTW - 2 * H)))
    dtb = pad_heads(dt_bias)
    alog_r = pad_heads(a_log)
    alog_c = alog_r.reshape(DEPTH, DTW, 1)
    dskip_x = jnp.repeat(d_skip, P, axis=2)
    norm_w = ssd_norm_w.reshape(DEPTH, 1, DI)
    gm = g_mix.reshape(DEPTH, 1, D)
    gf = g_ffn.reshape(DEPTH, 1, D)
    cbias = conv_b.reshape(DEPTH, 1, -1)
    wso = w_ssd_out.astype(BF)
    pw = pool_w.astype(BF)
    psc = pool_scale.reshape(DEPTH, 1, D)
    wpo = w_pool_out.astype(BF)
    wo = w_out.astype(BF)
    wgu = w_gate_up.astype(BF)
    wdn = w_down.astype(BF)
    gfin = g_final.reshape(1, D)

    xl = x.reshape(nb * L, D)
    xc_ = ctx.reshape(nb * Lc, D)
    tm_l = 1024
    tm_f = 512
    for l in range(DEPTH):
        last = l == DEPTH - 1

        def prepare(x2, row_fn, tm, seq):
            main, dt, dtt = _in_proj(x2, gm, mod3, l, row_fn, w_main, w_dt, dtb, tm)
            xcv = _conv(main, shift, conv_w, cbias, l, nb, seq, transpose=False)
            bt = _conv(main, shift, conv_w, cbias, l, nb, seq, transpose=True)
            return main, dt, dtt, xcv, bt

        def mix_and_ffn(x2, parts, ent, row_fn_mix, row_fn_ffn, seq, seg, final):
            main, dt, dtt, xcv, bt = parts
            y = _ssd_out(xcv, bt, dt, dtt, ent[0], ent[1], alog_r, alog_c, dskip_x, l, nb, seq)
            x2 = _mixer_out(y, main, x2, mod3, l, row_fn_mix, seg, norm_w, wso, pw, psc, wpo, wo)
            return _ffn(x2, gf, mod3, l, row_fn_ffn, wgu, wdn, gfin, tm_f, final)

        cparts = prepare(xc_, lambda i: ctx_row, 1024, Lc)
        centf, centb, cfin = _ssd_state(cparts[3], cparts[4], cparts[1], None, alog_r, expand2, l, nb, Lc)
        lparts = prepare(xl, lambda i: i // (L // tm_l), tm_l, L)
        lentf, lentb, _ = _ssd_state(lparts[3], lparts[4], lparts[1], cfin, alog_r, expand2, l, nb, L)
        xl = mix_and_ffn(xl, lparts, (lentf, lentb), lambda i: i // (L // (TM_MIX * MIX_SUB)),
                         lambda i: i // (L // tm_f), L, GRID_W, last)
        if not last:
            xc_ = mix_and_ffn(xc_, cparts, (centf, centb), lambda i: ctx_row, lambda i: ctx_row, Lc, Lc, False)
    return xl.reshape(nb, L, D)
```

```python
import functools

import numpy as np
import jax
import jax.numpy as jnp
from jax import lax
from jax.experimental import pallas as pl
from jax.experimental.pallas import tpu as pltpu

F32 = jnp.float32
BF = jnp.bfloat16

D = 1024
DEPTH = 2
H = 32
P = 64
G = 8
R = H // G
N = 128
Q = 128
DI = H * P
GN = G * N
GW = R * P
KC = 5
FH = 2816
POOL_WINDOWS = (2, 4, 8, 16)
NPW = len(POOL_WINDOWS)
PG = D // NPW
GRID_W = 64
EPS = 1e-6
LOG2E = 1.4426950408889634
DTW = 128

C_Z, C_X, C_B, C_C, C_POOL, C_GS, C_GP = 0, 2048, 4096, 5120, 6144, 7168, 8192
MAIN_W = 9216

VMEM_LIMIT = 56 * 1024 * 1024


def _dot(a, b):
    return jnp.dot(a, b, preferred_element_type=F32)


def _dot_nt(a, b):
    return lax.dot_general(a, b, (((1,), (1,)), ((), ())), preferred_element_type=F32)


def _split2(a):
    hi = a.astype(BF)
    return hi, (a - hi.astype(F32)).astype(BF)


def _split3(a):
    hi = a.astype(BF)
    r = a - hi.astype(F32)
    mid = r.astype(BF)
    lo = (r - mid.astype(F32)).astype(BF)
    return hi, mid, lo


def _dot3_rhs(m, a):
    hi, mid, lo = _split3(a)
    return (_dot(m, lo) + _dot(m, mid)) + _dot(m, hi)


def _dot3_lhs(a, m):
    hi, mid, lo = _split3(a)
    return (_dot(lo, m) + _dot(mid, m)) + _dot(hi, m)


def _silu(v):
    return v * jax.nn.sigmoid(v)


def _params(sem):
    return pltpu.CompilerParams(dimension_semantics=sem, vmem_limit_bytes=VMEM_LIMIT)


def _mod_spec(l, k, row_fn):
    return pl.BlockSpec((None, 1, D), lambda *ids: ((l * 8 + row_fn(*ids)) * 6 + k, 0, 0))


def _layer_spec(shape, l):
    zeros = (0,) * len(shape)
    return pl.BlockSpec((None,) + tuple(shape), lambda *ids: (l,) + zeros)


def _adaln_kernel(c_ref, w_ref, b_ref, o_ref):
    s_hi, s_lo = _split2(_silu(c_ref[...]))
    w_hi, w_lo = _split2(w_ref[...])
    acc = (_dot(s_lo, w_hi) + _dot(s_hi, w_lo)) + _dot(s_hi, w_hi)
    o_ref[...] = acc + b_ref[...]


def _adaln(cond8, w_ada, b_ada):
    tn = 1536
    return pl.pallas_call(
        _adaln_kernel,
        grid=(DEPTH, 6 * D // tn),
        in_specs=[
            pl.BlockSpec((8, D), lambda l, j: (0, 0)),
            pl.BlockSpec((None, D, tn), lambda l, j: (l, 0, j)),
            pl.BlockSpec((None, 1, tn), lambda l, j: (l, 0, j)),
        ],
        out_specs=pl.BlockSpec((None, 8, tn), lambda l, j: (l, 0, j)),
        out_shape=jax.ShapeDtypeStruct((DEPTH, 8, 6 * D), F32),
        compiler_params=_params(("arbitrary", "arbitrary")),
        name="adaln",
    )(cond8, w_ada, b_ada.reshape(DEPTH, 1, 6 * D))


def _inproj_kernel(x_ref, xn_ref, g_ref, sh_ref, sc_ref, shn_ref, scn_ref, w_ref, wdt_ref, dtb_ref,
                   o_ref, dt_ref, dtt_ref, h_ref, dts_ref, *, slab, nj):
    i = pl.program_id(0)
    j = pl.program_id(1)
    slot = i % 2
    tm, tn = o_ref.shape
    wh, wl = _split2(wdt_ref[...])
    wcat = jnp.concatenate([wh, wl], axis=0)

    def prologue(x, sh, sc):
        y = x * lax.rsqrt(jnp.mean(x * x, axis=-1, keepdims=True) + EPS)
        h = (y * g_ref[...]) * (1.0 + sc) + sh
        hb, hl = _split2(h)
        p = _dot_nt(hb, wcat)
        d = (_dot_nt(hl, wh) + p[:, DTW:]) + p[:, :DTW] + dtb_ref[...]
        return hb, jnp.maximum(d, 0.0) + jnp.log1p(jnp.exp(-jnp.abs(d)))

    @pl.when((i == 0) & (j == 0))
    def _():
        hb, dt = prologue(x_ref[...], sh_ref[...], sc_ref[...])
        h_ref[0] = hb
        dts_ref[0] = dt

    @pl.when(j == 0)
    def _():
        dt = dts_ref[slot]
        dt_ref[...] = dt
        dtt_ref[...] = dt.T

    def step(jj):
        rows = slice(min(jj * slab, tm - slab), min(jj * slab, tm - slab) + slab)
        hb, dt = prologue(xn_ref[rows, :], shn_ref[...], scn_ref[...])
        h_ref[1 - slot, rows, :] = hb
        dts_ref[1 - slot, rows, :] = dt
        v = _dot_nt(h_ref[slot], w_ref[0])
        lo, hi = jj * tn, (jj + 1) * tn
        n_silu = min(max(C_X - lo, 0), tn)
        n_sig = min(max(hi - C_GS, 0), tn)
        if n_silu:
            o_ref[:, :n_silu] = _silu(v[:, :n_silu]).astype(BF)
        if tn - n_silu - n_sig:
            o_ref[:, n_silu:tn - n_sig] = v[:, n_silu:tn - n_sig].astype(BF)
        if n_sig:
            o_ref[:, tn - n_sig:] = jax.nn.sigmoid(v[:, tn - n_sig:]).astype(BF)

    for jj in range(nj):
        pl.when(j == jj)(functools.partial(step, jj))


def _in_proj(x2, g, mod3, l, row_fn, w_main, w_dt, dt_bias, tm):
    T = x2.shape[0]
    n = T // tm
    tn = 3072
    nj = MAIN_W // tn
    slab = -(-tm // (nj * Q)) * Q
    nlow = C_POOL // tn
    w_row = lambda j: pl.multiple_of(jnp.where(j < nlow, j * tn, j * tn + 2 * H), 2 * H)
    nxt = lambda i: jnp.minimum(i + 1, n - 1)
    return pl.pallas_call(
        functools.partial(_inproj_kernel, slab=slab, nj=nj),
        grid=(n, nj),
        in_specs=[
            pl.BlockSpec((tm, D), lambda i, j: (0, 0)),
            pl.BlockSpec((tm, D), lambda i, j: (nxt(i), 0)),
            _layer_spec((1, D), l),
            _mod_spec(l, 0, lambda i, j: row_fn(0)),
            _mod_spec(l, 1, lambda i, j: row_fn(0)),
            _mod_spec(l, 0, lambda i, j: row_fn(nxt(i))),
            _mod_spec(l, 1, lambda i, j: row_fn(nxt(i))),
            pl.BlockSpec((pl.Element(1), pl.Element(tn), pl.Element(D)), lambda i, j: (l, w_row(j), 0)),
            _layer_spec((DTW, D), l),
            _layer_spec((1, DTW), l),
        ],
        out_specs=[
            pl.BlockSpec((tm, tn), lambda i, j: (i, j)),
            pl.BlockSpec((tm, DTW), lambda i, j: (i, 0)),
            pl.BlockSpec((DTW, tm), lambda i, j: (0, i)),
        ],
        out_shape=[
            jax.ShapeDtypeStruct((T, MAIN_W), BF),
            jax.ShapeDtypeStruct((T, DTW), F32),
            jax.ShapeDtypeStruct((DTW, T), F32),
        ],
        scratch_shapes=[pltpu.VMEM((2, tm, D), BF), pltpu.VMEM((2, tm, DTW), F32)],
        compiler_params=_params(("arbitrary", "arbitrary")),
        name="in_proj",
    )(x2, x2, g, mod3, mod3, mod3, mod3, w_main, w_dt, dt_bias)


CH = 64
SUBL = 8
CONV_SUB = 256


def _shift_table():
    s = np.zeros((Q // SUBL, KC - 1, SUBL, 2 * Q), np.float32)
    t = np.arange(Q)
    for si, k in enumerate([k for k in range(KC) if k != KC // 2]):
        s[t // SUBL, si, t % SUBL, CH + t + k - KC // 2] = 1.0
    return s.reshape((KC - 1) * Q, 2 * Q)


def _conv_kernel(prev_ref, cur_ref, next_ref, s_ref, w_ref, b_ref, o_ref, ext_ref, *, tl, nl, transpose):
    i = pl.program_id(1)
    prev = prev_ref[...]
    nxt = next_ref[...]
    ext_ref[0:CH, :] = jnp.where(i > 0, prev, jnp.zeros_like(prev))
    ext_ref[CH:CH + tl, :] = cur_ref[...]
    ext_ref[CH + tl:2 * CH + tl, :] = jnp.where(i < nl - 1, nxt, jnp.zeros_like(nxt))
    tc = cur_ref.shape[1]
    for r in range(tl // Q):
        for c0 in range(0, tc, CONV_SUB):
            cols = slice(c0, c0 + CONV_SUB)
            win = ext_ref[r * Q:(r + 2) * Q, cols]
            sh = _dot(s_ref[...], win)
            acc = b_ref[:, cols]
            si = 0
            for k in range(KC):
                if k == KC // 2:
                    tap = win[CH:CH + Q, :].astype(F32)
                else:
                    starts = [(tg * (KC - 1) + si) * SUBL for tg in range(Q // SUBL)]
                    tap = jnp.concatenate([sh[a:a + SUBL, :] for a in starts], axis=0)
                    si += 1
                acc = acc + w_ref[k:k + 1, cols] * tap
            out = _silu(acc)
            if transpose:
                o_ref[cols, r * Q:(r + 1) * Q] = out.T.astype(BF)
            else:
                o_ref[r * Q:(r + 1) * Q, cols] = out.astype(BF)


def _conv(main, shift, conv_w, conv_b, l, nb, L, transpose):
    T = nb * L
    tc = 1024
    tl = min(L, 1024)
    nl = L // tl
    rb = tl // CH
    last = T // CH - 1
    if transpose:
        ncol = GN // tc
        in_cb = lambda j: j + C_B // tc
        w_cb = lambda j: j + (C_B - C_X) // tc
        out_spec = pl.BlockSpec((tc, tl), lambda b, i, j: (j, b * nl + i))
        out_shape = jax.ShapeDtypeStruct((GN, T), BF)
    else:
        nx = DI // tc
        ncol = (DI + GN) // tc
        in_cb = lambda j: jnp.where(j < nx, j + C_X // tc, j - nx + C_C // tc)
        w_cb = lambda j: jnp.where(j < nx, j, j - nx + (C_C - C_X) // tc)
        out_spec = pl.BlockSpec((tl, tc), lambda b, i, j: (b * nl + i, j))
        out_shape = jax.ShapeDtypeStruct((T, DI + GN), BF)
    return pl.pallas_call(
        functools.partial(_conv_kernel, tl=tl, nl=nl, transpose=transpose),
        grid=(nb, nl, ncol),
        in_specs=[
            pl.BlockSpec((CH, tc), lambda b, i, j: (jnp.maximum((b * nl + i) * rb - 1, 0), in_cb(j))),
            pl.BlockSpec((tl, tc), lambda b, i, j: (b * nl + i, in_cb(j))),
            pl.BlockSpec((CH, tc), lambda b, i, j: (jnp.minimum((b * nl + i + 1) * rb, last), in_cb(j))),
            pl.BlockSpec(((KC - 1) * Q, 2 * Q), lambda b, i, j: (0, 0)),
            pl.BlockSpec((None, KC, tc), lambda b, i, j: (l, 0, w_cb(j))),
            pl.BlockSpec((None, 1, tc), lambda b, i, j: (l, 0, w_cb(j))),
        ],
        out_specs=out_spec,
        out_shape=out_shape,
        scratch_shapes=[pltpu.VMEM((tl + 2 * CH, tc), BF)],
        compiler_params=_params(("arbitrary", "arbitrary", "arbitrary")),
        name="conv_t" if transpose else "conv",
    )(main, main, main, shift, conv_w, conv_b)


def _tri(kind):
    r = lax.broadcasted_iota(jnp.int32, (Q, Q), 0)
    c = lax.broadcasted_iota(jnp.int32, (Q, Q), 1)
    return (c <= r) if kind == "le" else (c >= r)


def _expand_table():
    e = np.zeros((DTW, 2 * DI), np.float32)
    for k in range(2 * H):
        e[k, k * P:(k + 1) * P] = 1.0
    return np.concatenate([e, e], axis=0)


XROWS = 16


def _expand_lhs(w, dec):
    w_hi, w_lo = _split2(w)
    d_hi, d_mid, d_lo = [t.astype(F32) for t in _split3(jnp.broadcast_to(dec, (XROWS, DTW)))]
    rid = lax.broadcasted_iota(jnp.int32, (XROWS, DTW), 0)
    extra = jnp.where(rid == 0, d_hi, jnp.where(rid == 1, d_mid, jnp.where(rid == 2, d_lo, 0.0))).astype(BF)
    zero = jnp.zeros((XROWS, DTW), BF)
    return jnp.concatenate([jnp.concatenate([w_hi, w_lo], axis=1), jnp.concatenate([extra, zero], axis=1)], axis=0)


MAX_CPS = 4


def _ssd_state_kernel(*refs, nc, has_init):
    CPS = refs[-1].shape[0]
    if has_init:
        (xf_ref, btf_ref, dtf_ref, xb_ref, btb_ref, dtb_ref, init_ref, alr_ref, e2_ref,
         entf_ref, entb_ref, fin_ref, st_ref, xw_ref, dec_ref) = refs
    else:
        (xf_ref, btf_ref, dtf_ref, xb_ref, btb_ref, dtb_ref, alr_ref, e2_ref,
         entf_ref, entb_ref, fin_ref, st_ref, xw_ref, dec_ref) = refs
    c = pl.program_id(1)

    @pl.when(c == 0)
    def _():
        st_ref[...] = init_ref[...] if has_init else jnp.zeros_like(st_ref)

    a_row = -jnp.exp(alr_ref[...])
    dirs = ((xf_ref, btf_ref, dtf_ref, entf_ref), (xb_ref, btb_ref, dtb_ref, entb_ref))
    order = [[(s, s) for s in range(CPS)], [(s, CPS - 1 - s) for s in range(CPS)]]
    for d, (x_ref, bt_ref, dt_ref, ent_ref) in enumerate(dirs):
        for s, ck in order[d]:
            rows = slice(ck * Q, (ck + 1) * Q)
            dt = dt_ref[rows, :]
            acum = _dot3_rhs(_tri("le" if d == 0 else "ge").astype(BF), dt * a_row)
            tot = acum[Q - 1:Q] if d == 0 else acum[0:1]
            ex = _dot(_expand_lhs(dt * jnp.exp(tot - acum), jnp.exp(tot)), e2_ref[:, d * DI:(d + 1) * DI])
            dec_ref[s, d] = (ex[Q + 2:Q + 3] + ex[Q + 1:Q + 2]) + ex[Q:Q + 1]
            xw_ref[s, d] = x_ref[rows, :] * ex[0:Q].astype(BF)
    for s in range(CPS):
        for d, (x_ref, bt_ref, dt_ref, ent_ref) in enumerate(dirs):
            ck = order[d][s][1]
            for g in range(G):
                cols = slice(g * GW, (g + 1) * GW)
                upd = _dot(bt_ref[g * N:(g + 1) * N, ck * Q:(ck + 1) * Q], xw_ref[s, d, :, cols])
                st = st_ref[d, g]
                ent_ref[ck, g] = st.astype(BF)
                st_ref[d, g] = st * dec_ref[s, d, :, cols] + upd

    @pl.when(c == nc - 1)
    def _():
        fin_ref[...] = st_ref[...]


def _ssd_state(xc, bt, dt, init, alog_r, expand2, l, nb, L):
    CPS = min(MAX_CPS, L // Q)
    nc = L // (CPS * Q)
    fw = lambda b, c: b * nc + c
    bw = lambda b, c: b * nc + (nc - 1 - c)
    has_init = init is not None
    chunk_specs = lambda ch: [
        pl.BlockSpec((CPS * Q, DI), lambda b, c: (ch(b, c), 0)),
        pl.BlockSpec((GN, CPS * Q), lambda b, c: (0, ch(b, c))),
        pl.BlockSpec((CPS * Q, DTW), lambda b, c: (ch(b, c), 0)),
    ]
    st_spec = pl.BlockSpec((None, 2, G, N, GW), lambda b, c: (b, 0, 0, 0, 0))
    in_specs = chunk_specs(fw) + chunk_specs(bw) + ([st_spec] if has_init else []) + [
        _layer_spec((1, DTW), l),
        pl.BlockSpec((2 * DTW, 2 * DI), lambda b, c: (0, 0)),
    ]
    args = (xc, bt, dt, xc, bt, dt) + ((init,) if has_init else ()) + (alog_r, expand2)
    return pl.pallas_call(
        functools.partial(_ssd_state_kernel, nc=nc, has_init=has_init),
        grid=(nb, nc),
        in_specs=in_specs,
        out_specs=[
            pl.BlockSpec((None, CPS, G, N, GW), lambda b, c: (b, c, 0, 0, 0)),
            pl.BlockSpec((None, CPS, G, N, GW), lambda b, c: (b, nc - 1 - c, 0, 0, 0)),
            st_spec,
        ],
        out_shape=[
            jax.ShapeDtypeStruct((nb, nc * CPS, G, N, GW), BF),
            jax.ShapeDtypeStruct((nb, nc * CPS, G, N, GW), BF),
            jax.ShapeDtypeStruct((nb, 2, G, N, GW), F32),
        ],
        scratch_shapes=[pltpu.VMEM((2, G, N, GW), F32), pltpu.VMEM((CPS, 2, Q, DI), BF),
                        pltpu.VMEM((CPS, 2, 1, DI), F32)],
        compiler_params=_params(("arbitrary", "arbitrary")),
        name="ssd_state",
    )(*args)


OUT_CPS = 4


def _fill_blockdiag(bd_ref, xg):
    blk = lax.broadcasted_iota(jnp.int32, (Q, GW), 1) // P
    for r in range(R):
        bd_ref[r * Q:(r + 1) * Q, :] = jnp.where(blk == r, xg, jnp.zeros_like(xg))


def _ssd_out_kernel(xc_ref, bt_ref, dt_ref, dtt_ref, entf_ref, entb_ref,
                    alr_ref, alc_ref, dsk_ref, o_ref, bd_ref):
    x_ref = xc_ref.at[:, 0:DI]
    c_ref = xc_ref.at[:, DI:DI + GN]
    a_row = -jnp.exp(alr_ref[...])
    a_col = -jnp.exp(alc_ref[...])
    left = lax.broadcasted_iota(jnp.int32, (Q, 2 * P), 1) < P
    mask = [_tri("le"), _tri("ge")]
    ri = lax.broadcasted_iota(jnp.int32, (Q, Q), 0)
    ci = lax.broadcasted_iota(jnp.int32, (Q, Q), 1)
    below, above = ci < ri, ci > ri
    ents = (entf_ref, entb_ref)
    dsum = dsk_ref[0:1, :] + dsk_ref[1:2, :]
    for s in range(o_ref.shape[0] // Q):
        rows = slice(s * Q, (s + 1) * Q)
        dt = dt_ref[rows, :]
        dtt = dtt_ref[:, rows]
        log_dtt = jnp.log(dtt)
        rowd = jnp.log(dtt[0:H] + dtt[H:2 * H]) * LOG2E
        acum, rowv = [], []
        for d in range(2):
            acum.append(_dot3_rhs(mask[d].astype(BF), dt * a_row) * LOG2E)
            acum_t = _dot3_lhs(dtt * a_col, mask[1 - d].astype(BF))
            rowv.append((log_dtt - acum_t) * LOG2E)
        for g in range(G):
            cols = slice(g * GW, (g + 1) * GW)
            xg = x_ref[rows, cols]
            cg = c_ref[rows, g * N:(g + 1) * N]
            cbb = _dot(cg, bt_ref[g * N:(g + 1) * N, rows]).astype(BF)
            _fill_blockdiag(bd_ref.at[s, g], xg)
            yin = [_dot(cg, ents[d][s, g]) for d in range(2)]
            ms, yi = [], []
            for r0 in range(0, R, 2):
                es = ([], [])
                for h in (g * R + r0, g * R + r0 + 1):
                    acols = [jnp.broadcast_to(acum[d][:, d * H + h:d * H + h + 1], (Q, Q)) for d in range(2)]
                    expo = jnp.where(below, acols[0] + rowv[0][h:h + 1, :],
                                     jnp.where(above, acols[1] + rowv[1][H + h:H + h + 1, :], rowd[h:h + 1, :]))
                    ms.append(cbb * jnp.exp2(expo).astype(BF))
                    for d in range(2):
                        es[d].append(jnp.exp2(acols[d]))
                pc = slice(r0 * P, (r0 + 2) * P)
                yi.append(yin[0][:, pc] * jnp.where(left, es[0][0], es[0][1])
                          + yin[1][:, pc] * jnp.where(left, es[1][0], es[1][1]))
            y = dsum[:, cols] * xg.astype(F32) + _dot(jnp.concatenate(ms, axis=1), bd_ref[s, g])
            o_ref[rows, cols] = (y + jnp.concatenate(yi, axis=1)).astype(BF)


def _ssd_out(xc, bt, dt, dtt, entf, entb, alog_r, alog_c, dskip_x, l, nb, L):
    cps = min(OUT_CPS, L // Q)
    rows = cps * Q
    nc = L // rows
    T = nb * L
    ch = lambda b, c: b * nc + c
    ent_spec = pl.BlockSpec((None, cps, G, N, GW), lambda b, c: (b, c, 0, 0, 0))
    return pl.pallas_call(
        _ssd_out_kernel,
        grid=(nb, nc),
        in_specs=[
            pl.BlockSpec((rows, DI + GN), lambda b, c: (ch(b, c), 0)),
            pl.BlockSpec((GN, rows), lambda b, c: (0, ch(b, c))),
            pl.BlockSpec((rows, DTW), lambda b, c: (ch(b, c), 0)),
            pl.BlockSpec((DTW, rows), lambda b, c: (0, ch(b, c))),
            ent_spec,
            ent_spec,
            _layer_spec((1, DTW), l),
            _layer_spec((DTW, 1), l),
            _layer_spec((2, DI), l),
        ],
        out_specs=pl.BlockSpec((rows, DI), lambda b, c: (ch(b, c), 0)),
        out_shape=jax.ShapeDtypeStruct((T, DI), BF),
        scratch_shapes=[pltpu.VMEM((cps, G, R * Q, GW), BF)],
        compiler_params=_params(("parallel", "parallel")),
        name="ssd_out",
    )(xc, bt, dt, dtt, entf, entb, alog_r, alog_c, dskip_x)


TM_MIX = 256
MIX_SUB = 4


def _mixer_kernel(y_ref, z_ref, pgg_ref, x_ref, ga_ref, nw_ref, band_ref, cnt_ref, wso_ref,
                  pw_ref, psc_ref, wpo_ref, wo_ref, o_ref):
    u_ref = pgg_ref.at[:, 0:D]
    gs_ref = pgg_ref.at[:, C_GS - C_POOL:C_GS - C_POOL + D]
    gp_ref = pgg_ref.at[:, C_GP - C_POOL:C_GP - C_POOL + D]
    for s in range(MIX_SUB):
        rows = slice(s * TM_MIX, (s + 1) * TM_MIX)
        yns = []
        for g in range(G):
            cols = slice(g * GW, (g + 1) * GW)
            yz = (y_ref[rows, cols] * z_ref[rows, cols]).astype(F32)
            yn = yz * lax.rsqrt(jnp.mean(yz * yz, axis=-1, keepdims=True) + EPS)
            yns.append((yn * nw_ref[:, cols]).astype(BF))
        o_ssd = _dot(jnp.concatenate(yns, axis=1), wso_ref[...])
        pms = []
        for gi in range(NPW):
            ug = u_ref[rows, gi * PG:(gi + 1) * PG]
            wsum = _dot(band_ref[gi], ug)
            pm = wsum / cnt_ref[:, gi:gi + 1] - ug.astype(F32)
            pms.append(_dot(pm.astype(BF), pw_ref[gi]))
        pmc = (jnp.concatenate(pms, axis=1) * psc_ref[...]).astype(BF)
        o_pool = _dot(pmc, wpo_ref[...])
        mix = gs_ref[rows, :].astype(F32) * o_ssd + gp_ref[rows, :].astype(F32) * o_pool
        o_ref[rows, :] = x_ref[rows, :] + ga_ref[...] * _dot(mix.astype(BF), wo_ref[...])


def _pool_tables(seg):
    t = np.arange(TM_MIX)
    same = (t[:, None] // seg) == (t[None, :] // seg)
    ts = t % seg
    bands, cnts = [], []
    for k in POOL_WINDOWS:
        lo = t[:, None] - k // 2
        hi = t[:, None] + k // 2
        bands.append(same & (t[None, :] >= lo) & (t[None, :] < hi))
        cnts.append(np.minimum(ts + k // 2, seg) - np.maximum(ts - k // 2, 0))
    cnt = np.ones((TM_MIX, 128), np.float32)
    cnt[:, :NPW] = np.stack(cnts, axis=1)
    return jnp.asarray(np.stack(bands).astype(np.float32), BF), jnp.asarray(cnt)


def _mixer_out(y, main, x2, mod3, l, row_fn, seg, norm_w, w_ssd_out, pool_w, pool_scale, w_pool_out, w_out):
    T = x2.shape[0]
    tm = TM_MIX * MIX_SUB
    band, cnt = _pool_tables(seg)
    return pl.pallas_call(
        _mixer_kernel,
        grid=(T // tm,),
        in_specs=[
            pl.BlockSpec((tm, DI), lambda i: (i, 0)),
            pl.BlockSpec((tm, DI), lambda i: (i, C_Z // DI)),
            pl.BlockSpec((tm, MAIN_W - C_POOL), lambda i: (i, C_POOL // (MAIN_W - C_POOL))),
            pl.BlockSpec((tm, D), lambda i: (i, 0)),
            _mod_spec(l, 2, row_fn),
            _layer_spec((1, DI), l),
            pl.BlockSpec((NPW, TM_MIX, TM_MIX), lambda i: (0, 0, 0)),
            pl.BlockSpec((TM_MIX, 128), lambda i: (0, 0)),
            _layer_spec((DI, D), l),
            _layer_spec((NPW, PG, PG), l),
            _layer_spec((1, D), l),
            _layer_spec((D, D), l),
            _layer_spec((D, D), l),
        ],
        out_specs=pl.BlockSpec((tm, D), lambda i: (i, 0)),
        out_shape=jax.ShapeDtypeStruct((T, D), F32),
        compiler_params=_params(("parallel",)),
        name="mixer_out",
    )(y, main, main, x2, mod3, norm_w, band, cnt, w_ssd_out, pool_w, pool_scale, w_pool_out, w_out)


FFN_CHUNKS = (768, 768, 768, 512)


def _ffn_kernel(x_ref, xn_ref, g_ref, sh_ref, sc_ref, ga_ref, shn_ref, scn_ref, wg_ref, wu_ref, wd_ref, gf_ref,
                o_ref, h_ref, *, final):
    i = pl.program_id(0)
    slot = i % 2

    def norm_mod(x, sh, sc):
        y = x * lax.rsqrt(jnp.mean(x * x, axis=-1, keepdims=True) + EPS)
        return ((y * g_ref[...]) * (1.0 + sc) + sh).astype(BF)

    @pl.when(i == 0)
    def _():
        h_ref[0] = norm_mod(x_ref[...], sh_ref[...], sc_ref[...])

    h_ref[1 - slot] = norm_mod(xn_ref[...], shn_ref[...], scn_ref[...])
    h = h_ref[slot]
    acc = None
    c0 = 0
    for cw in FFN_CHUNKS:
        act = (_silu(_dot(h, wg_ref[:, c0:c0 + cw])) * _dot(h, wu_ref[:, c0:c0 + cw])).astype(BF)
        part = _dot(act, wd_ref[c0:c0 + cw, :])
        acc = part if acc is None else acc + part
        c0 += cw
    xn = x_ref[...] + ga_ref[...] * acc
    if final:
        xn = (xn * lax.rsqrt(jnp.mean(xn * xn, axis=-1, keepdims=True) + EPS)) * gf_ref[...]
    o_ref[...] = xn


def _ffn(x2, g, mod3, l, row_fn, w_gate_up, w_down, g_final, tm, final):
    assert sum(FFN_CHUNKS) == FH
    T = x2.shape[0]
    n = T // tm
    nxt = lambda i: jnp.minimum(i + 1, n - 1)
    once = pl.Buffered(1)
    return pl.pallas_call(
        functools.partial(_ffn_kernel, final=final),
        grid=(n,),
        in_specs=[
            pl.BlockSpec((tm, D), lambda i: (i, 0)),
            pl.BlockSpec((tm, D), lambda i: (nxt(i), 0)),
            _layer_spec((1, D), l),
            _mod_spec(l, 3, row_fn),
            _mod_spec(l, 4, row_fn),
            _mod_spec(l, 5, row_fn),
            _mod_spec(l, 3, lambda i: row_fn(nxt(i))),
            _mod_spec(l, 4, lambda i: row_fn(nxt(i))),
            pl.BlockSpec((None, D, FH), lambda i: (l, 0, 0), pipeline_mode=once),
            pl.BlockSpec((None, D, FH), lambda i: (l, 0, 1), pipeline_mode=once),
            pl.BlockSpec((None, FH, D), lambda i: (l, 0, 0), pipeline_mode=once),
            pl.BlockSpec((1, D), lambda i: (0, 0)),
        ],
        out_specs=pl.BlockSpec((tm, D), lambda i: (i, 0)),
        out_shape=jax.ShapeDtypeStruct((T, D), F32),
        scratch_shapes=[pltpu.VMEM((2, tm, D), BF)],
        compiler_params=_params(("arbitrary",)),
        name="ffn",
    )(x2, x2, g, mod3, mod3, mod3, mod3, mod3, w_gate_up, w_gate_up, w_down, g_final)


def kernel(x, c, ctx, c_ctx, w_ada, b_ada, g_mix, w_in, conv_w, conv_b, dt_bias, a_log, d_skip, ssd_norm_w,
           w_ssd_out, pool_w, pool_scale, w_pool_out, w_out, g_ffn, w_gate_up, w_down, g_final):
    nb, L, _ = x.shape
    Lc = ctx.shape[1]
    ctx_row = nb

    cond8 = jnp.zeros((8, D), F32).at[:nb].set(c).at[ctx_row].set(c_ctx)
    mod3 = _adaln(cond8, w_ada, b_ada).reshape(DEPTH * 8 * 6, 1, D)

    shift = jnp.asarray(_shift_table(), BF)
    expand2 = jnp.asarray(_expand_table(), BF)

    w_in_t = jnp.swapaxes(w_in, 1, 2)
    w_main = w_in_t.astype(BF)
    w_dt = jnp.pad(w_in_t[:, C_POOL:C_POOL + 2 * H], ((0, 0), (0, DTW - 2 * H), (0, 0)))
    pad_heads = lambda a: jnp.pad(a.reshape(DEPTH, 1, 2 * H), ((0, 0), (0, 0), (0, DTW - 2 * H)))
    dtb = pad_heads(dt_bias)
    alog_r = pad_heads(a_log)
    alog_c = alog_r.reshape(DEPTH, DTW, 1)
    dskip_x = jnp.repeat(d_skip, P, axis=2)
    norm_w = ssd_norm_w.reshape(DEPTH, 1, DI)
    gm = g_mix.reshape(DEPTH, 1, D)
    gf = g_ffn.reshape(DEPTH, 1, D)
    cbias = conv_b.reshape(DEPTH, 1, -1)
    wso = w_ssd_out.astype(BF)
    pw = pool_w.astype(BF)
    psc = pool_scale.reshape(DEPTH, 1, D)
    wpo = w_pool_out.astype(BF)
    wo = w_out.astype(BF)
    wgu = w_gate_up.astype(BF)
    wdn = w_down.astype(BF)
    gfin = g_final.reshape(1, D)

    xl = x.reshape(nb * L, D)
    xc_ = ctx.reshape(nb * Lc, D)
    tm_l = 1024
    tm_f = 512
    for l in range(DEPTH):
        last = l == DEPTH - 1

        def prepare(x2, row_fn, tm, seq):
            main, dt, dtt = _in_proj(x2, gm, mod3, l, row_fn, w_main, w_dt, dtb, tm)
            xcv = _conv(main, shift, conv_w, cbias, l, nb, seq, transpose=False)
            bt = _conv(main, shift, conv_w, cbias, l, nb, seq, transpose=True)
            return main, dt, dtt, xcv, bt

        def mix_and_ffn(x2, parts, ent, row_fn_mix, row_fn_ffn, seq, seg, final):
            main, dt, dtt, xcv, bt = parts
            y = _ssd_out(xcv, bt, dt, dtt, ent[0], ent[1], alog_r, alog_c, dskip_x, l, nb, seq)
            x2 = _mixer_out(y, main, x2, mod3, l, row_fn_mix, seg, norm_w, wso, pw, psc, wpo, wo)
            return _ffn(x2, gf, mod3, l, row_fn_ffn, wgu, wdn, gfin, tm_f, final)

        cparts = prepare(xc_, lambda i: ctx_row, 1024, Lc)
        centf, centb, cfin = _ssd_state(cparts[3], cparts[4], cparts[1], None, alog_r, expand2, l, nb, Lc)
        lparts = prepare(xl, lambda i: i // (L // tm_l), tm_l, L)
        lentf, lentb, _ = _ssd_state(lparts[3], lparts[4], lparts[1], cfin, alog_r, expand2, l, nb, L)
        xl = mix_and_ffn(xl, lparts, (lentf, lentb), lambda i: i // (L // (TM_MIX * MIX_SUB)),
                         lambda i: i // (L // tm_f), L, GRID_W, last)
        if not last:
            xc_ = mix_and_ffn(xc_, cparts, (centf, centb), lambda i: ctx_row, lambda i: ctx_row, Lc, Lc, False)
    return xl.reshape(nb, L, D)
```

```python
import functools

import numpy as np
import jax
import jax.numpy as jnp
from jax import lax
from jax.experimental import pallas as pl
from jax.experimental.pallas import tpu as pltpu

F32 = jnp.float32
BF = jnp.bfloat16

D = 1024
DEPTH = 2
H = 32
P = 64
G = 8
R = H // G
N = 128
Q = 128
DI = H * P
GN = G * N
GW = R * P
KC = 5
FH = 2816
POOL_WINDOWS = (2, 4, 8, 16)
NPW = len(POOL_WINDOWS)
PG = D // NPW
GRID_W = 64
EPS = 1e-6
LOG2E = 1.4426950408889634
DTW = 128

C_Z, C_X, C_B, C_C, C_POOL, C_GS, C_GP = 0, 2048, 4096, 5120, 6144, 7168, 8192
MAIN_W = 9216

LANES = 128
VMEM_LIMIT = 56 * 1024 * 1024

ADALN_TN = 1536
IN_TM, IN_TN = 1024, 3072
CONV_TL, CONV_TC = 1024, 1024
FFN_TM = 512


def _dot(a, b):
    return jnp.dot(a, b, preferred_element_type=F32)


def _dot_nt(a, b):
    return lax.dot_general(a, b, (((1,), (1,)), ((), ())), preferred_element_type=F32)


def _split2(a):
    hi = a.astype(BF)
    return hi, (a - hi.astype(F32)).astype(BF)


def _split3(a):
    hi = a.astype(BF)
    r = a - hi.astype(F32)
    mid = r.astype(BF)
    lo = (r - mid.astype(F32)).astype(BF)
    return hi, mid, lo


def _dot3_rhs(m, a):
    hi, mid, lo = _split3(a)
    return (_dot(m, lo) + _dot(m, mid)) + _dot(m, hi)


def _dot3_lhs(a, m):
    hi, mid, lo = _split3(a)
    return (_dot(lo, m) + _dot(mid, m)) + _dot(hi, m)


def _sigmoid(v):
    return 0.5 * jnp.tanh(0.5 * v) + 0.5


def _silu(v):
    hv = 0.5 * v
    return hv * jnp.tanh(hv) + hv


def _params(sem):
    return pltpu.CompilerParams(dimension_semantics=sem, vmem_limit_bytes=VMEM_LIMIT)


def _mod_spec(l, k, row_fn):
    return pl.BlockSpec((None, 1, D), lambda *ids: ((l * 8 + row_fn(*ids)) * 6 + k, 0, 0))


def _layer_spec(shape, l):
    zeros = (0,) * len(shape)
    return pl.BlockSpec((None,) + tuple(shape), lambda *ids: (l,) + zeros)


def _adaln_kernel(c_ref, w_ref, b_ref, o_ref):
    s_hi, s_lo = _split2(_silu(c_ref[...]))
    w_hi, w_lo = _split2(w_ref[...])
    acc = (_dot(s_lo, w_hi) + _dot(s_hi, w_lo)) + _dot(s_hi, w_hi)
    o_ref[...] = acc + b_ref[...]


def _adaln(cond8, w_ada, b_ada):
    tn = ADALN_TN
    return pl.pallas_call(
        _adaln_kernel,
        grid=(DEPTH, 6 * D // tn),
        in_specs=[
            pl.BlockSpec((8, D), lambda l, j: (0, 0)),
            pl.BlockSpec((None, D, tn), lambda l, j: (l, 0, j)),
            pl.BlockSpec((None, 1, tn), lambda l, j: (l, 0, j)),
        ],
        out_specs=pl.BlockSpec((None, 8, tn), lambda l, j: (l, 0, j)),
        out_shape=jax.ShapeDtypeStruct((DEPTH, 8, 6 * D), F32),
        compiler_params=_params(("arbitrary", "arbitrary")),
        name="adaln",
    )(cond8, w_ada, b_ada.reshape(DEPTH, 1, 6 * D))


def _inproj_kernel(x_ref, xn_ref, g_ref, sh_ref, sc_ref, shn_ref, scn_ref, w_ref, wdt_ref, dtb_ref,
                   o_ref, dt_ref, dtt_ref, h_ref, dts_ref, *, slab, nj):
    i = pl.program_id(0)
    j = pl.program_id(1)
    slot = i % 2
    tm, tn = o_ref.shape
    wh, wl = _split2(wdt_ref[...])
    wcat = jnp.concatenate([wh, wl], axis=0)

    def prologue(x, sh, sc):
        y = x * lax.rsqrt(jnp.mean(x * x, axis=-1, keepdims=True) + EPS)
        h = (y * g_ref[...]) * (1.0 + sc) + sh
        hb, hl = _split2(h)
        p = _dot_nt(hb, wcat)
        d = (_dot_nt(hl, wh) + p[:, DTW:]) + p[:, :DTW] + dtb_ref[...]
        return hb, jnp.maximum(d, 0.0) + jnp.log1p(jnp.exp(-jnp.abs(d)))

    @pl.when((i == 0) & (j == 0))
    def _():
        hb, dt = prologue(x_ref[...], sh_ref[...], sc_ref[...])
        h_ref[0] = hb
        dts_ref[0] = dt

    @pl.when(j == 0)
    def _():
        dt = dts_ref[slot]
        dt_ref[...] = dt
        dtt_ref[...] = dt.T

    def step(jj):
        rows = slice(min(jj * slab, tm - slab), min(jj * slab, tm - slab) + slab)
        hb, dt = prologue(xn_ref[rows, :], shn_ref[...], scn_ref[...])
        h_ref[1 - slot, rows, :] = hb
        dts_ref[1 - slot, rows, :] = dt
        v = _dot_nt(h_ref[slot], w_ref[0])
        lo, hi = jj * tn, (jj + 1) * tn
        n_silu = min(max(C_X - lo, 0), tn)
        n_sig = min(max(hi - C_GS, 0), tn)
        if n_silu:
            o_ref[:, :n_silu] = _silu(v[:, :n_silu]).astype(BF)
        if tn - n_silu - n_sig:
            o_ref[:, n_silu:tn - n_sig] = v[:, n_silu:tn - n_sig].astype(BF)
        if n_sig:
            o_ref[:, tn - n_sig:] = _sigmoid(v[:, tn - n_sig:]).astype(BF)

    for jj in range(nj):
        pl.when(j == jj)(functools.partial(step, jj))


def _in_proj(x2, g, mod3, l, row_fn, w_main, w_dt, dt_bias, tm):
    T = x2.shape[0]
    n = T // tm
    tn = IN_TN
    nj = MAIN_W // tn
    slab = -(-tm // (nj * Q)) * Q
    nlow = C_POOL // tn
    w_row = lambda j: pl.multiple_of(jnp.where(j < nlow, j * tn, j * tn + 2 * H), 2 * H)
    nxt = lambda i: jnp.minimum(i + 1, n - 1)
    return pl.pallas_call(
        functools.partial(_inproj_kernel, slab=slab, nj=nj),
        grid=(n, nj),
        in_specs=[
            pl.BlockSpec((tm, D), lambda i, j: (0, 0)),
            pl.BlockSpec((tm, D), lambda i, j: (nxt(i), 0)),
            _layer_spec((1, D), l),
            _mod_spec(l, 0, lambda i, j: row_fn(0)),
            _mod_spec(l, 1, lambda i, j: row_fn(0)),
            _mod_spec(l, 0, lambda i, j: row_fn(nxt(i))),
            _mod_spec(l, 1, lambda i, j: row_fn(nxt(i))),
            pl.BlockSpec((pl.Element(1), pl.Element(tn), pl.Element(D)), lambda i, j: (l, w_row(j), 0)),
            _layer_spec((DTW, D), l),
            _layer_spec((1, DTW), l),
        ],
        out_specs=[
            pl.BlockSpec((tm, tn), lambda i, j: (i, j)),
            pl.BlockSpec((tm, DTW), lambda i, j: (i, 0)),
            pl.BlockSpec((DTW, tm), lambda i, j: (0, i)),
        ],
        out_shape=[
            jax.ShapeDtypeStruct((T, MAIN_W), BF),
            jax.ShapeDtypeStruct((T, DTW), F32),
            jax.ShapeDtypeStruct((DTW, T), F32),
        ],
        scratch_shapes=[pltpu.VMEM((2, tm, D), BF), pltpu.VMEM((2, tm, DTW), F32)],
        compiler_params=_params(("arbitrary", "arbitrary")),
        name="in_proj",
    )(x2, x2, g, mod3, mod3, mod3, mod3, w_main, w_dt, dt_bias)


CH = 64
SUBL = 8
CONV_SUB = 256


def _shift_table():
    s = np.zeros((Q // SUBL, KC - 1, SUBL, 2 * Q), np.float32)
    t = np.arange(Q)
    for si, k in enumerate([k for k in range(KC) if k != KC // 2]):
        s[t // SUBL, si, t % SUBL, CH + t + k - KC // 2] = 1.0
    return s.reshape((KC - 1) * Q, 2 * Q)


def _conv_kernel(prev_ref, cur_ref, next_ref, s_ref, w_ref, b_ref, o_ref, ext_ref, *, tl, nl, transpose):
    i = pl.program_id(1)
    prev = prev_ref[...]
    nxt = next_ref[...]
    ext_ref[0:CH, :] = jnp.where(i > 0, prev, jnp.zeros_like(prev))
    ext_ref[CH:CH + tl, :] = cur_ref[...]
    ext_ref[CH + tl:2 * CH + tl, :] = jnp.where(i < nl - 1, nxt, jnp.zeros_like(nxt))
    tc = cur_ref.shape[1]
    for r in range(tl // Q):
        for c0 in range(0, tc, CONV_SUB):
            cols = slice(c0, c0 + CONV_SUB)
            win = ext_ref[r * Q:(r + 2) * Q, cols]
            sh = _dot(s_ref[...], win)
            acc = b_ref[:, cols]
            si = 0
            for k in range(KC):
                if k == KC // 2:
                    tap = win[CH:CH + Q, :].astype(F32)
                else:
                    starts = [(tg * (KC - 1) + si) * SUBL for tg in range(Q // SUBL)]
                    tap = jnp.concatenate([sh[a:a + SUBL, :] for a in starts], axis=0)
                    si += 1
                acc = acc + w_ref[k:k + 1, cols] * tap
            out = _silu(acc)
            if transpose:
                o_ref[cols, r * Q:(r + 1) * Q] = out.T.astype(BF)
            else:
                o_ref[r * Q:(r + 1) * Q, cols] = out.astype(BF)


def _conv(main, shift, conv_w, conv_b, l, nb, L, transpose):
    T = nb * L
    tc = CONV_TC
    tl = min(L, CONV_TL)
    nl = L // tl
    rb = tl // CH
    last = T // CH - 1
    if transpose:
        ncol = GN // tc
        in_cb = lambda j: j + C_B // tc
        w_cb = lambda j: j + (C_B - C_X) // tc
        out_spec = pl.BlockSpec((tc, tl), lambda b, i, j: (j, b * nl + i))
        out_shape = jax.ShapeDtypeStruct((GN, T), BF)
    else:
        nx = DI // tc
        ncol = (DI + GN) // tc
        in_cb = lambda j: jnp.where(j < nx, j + C_X // tc, j - nx + C_C // tc)
        w_cb = lambda j: jnp.where(j < nx, j, j - nx + (C_C - C_X) // tc)
        out_spec = pl.BlockSpec((tl, tc), lambda b, i, j: (b * nl + i, j))
        out_shape = jax.ShapeDtypeStruct((T, DI + GN), BF)
    return pl.pallas_call(
        functools.partial(_conv_kernel, tl=tl, nl=nl, transpose=transpose),
        grid=(nb, nl, ncol),
        in_specs=[
            pl.BlockSpec((CH, tc), lambda b, i, j: (jnp.maximum((b * nl + i) * rb - 1, 0), in_cb(j))),
            pl.BlockSpec((tl, tc), lambda b, i, j: (b * nl + i, in_cb(j))),
            pl.BlockSpec((CH, tc), lambda b, i, j: (jnp.minimum((b * nl + i + 1) * rb, last), in_cb(j))),
            pl.BlockSpec(((KC - 1) * Q, 2 * Q), lambda b, i, j: (0, 0)),
            pl.BlockSpec((None, KC, tc), lambda b, i, j: (l, 0, w_cb(j))),
            pl.BlockSpec((None, 1, tc), lambda b, i, j: (l, 0, w_cb(j))),
        ],
        out_specs=out_spec,
        out_shape=out_shape,
        scratch_shapes=[pltpu.VMEM((tl + 2 * CH, tc), BF)],
        compiler_params=_params(("arbitrary", "arbitrary", "arbitrary")),
        name="conv_t" if transpose else "conv",
    )(main, main, main, shift, conv_w, conv_b)


def _tri(kind):
    r = lax.broadcasted_iota(jnp.int32, (Q, Q), 0)
    c = lax.broadcasted_iota(jnp.int32, (Q, Q), 1)
    return (c <= r) if kind == "le" else (c >= r)


def _expand_table():
    e = np.zeros((DTW, 2 * DI), np.float32)
    for k in range(2 * H):
        e[k, k * P:(k + 1) * P] = 1.0
    return np.concatenate([e, e], axis=0)


XROWS = 16


def _expand_lhs(w, dec):
    w_hi, w_lo = _split2(w)
    d_hi, d_mid, d_lo = [t.astype(F32) for t in _split3(jnp.broadcast_to(dec, (XROWS, DTW)))]
    rid = lax.broadcasted_iota(jnp.int32, (XROWS, DTW), 0)
    extra = jnp.where(rid == 0, d_hi, jnp.where(rid == 1, d_mid, jnp.where(rid == 2, d_lo, 0.0))).astype(BF)
    zero = jnp.zeros((XROWS, DTW), BF)
    return jnp.concatenate([jnp.concatenate([w_hi, w_lo], axis=1), jnp.concatenate([extra, zero], axis=1)], axis=0)


MAX_CPS = 4


def _ssd_state_kernel(*refs, nc, has_init):
    CPS = refs[-1].shape[0]
    if has_init:
        (xf_ref, btf_ref, dtf_ref, xb_ref, btb_ref, dtb_ref, init_ref, alr_ref, e2_ref,
         entf_ref, entb_ref, fin_ref, st_ref, xw_ref, dec_ref) = refs
    else:
        (xf_ref, btf_ref, dtf_ref, xb_ref, btb_ref, dtb_ref, alr_ref, e2_ref,
         entf_ref, entb_ref, fin_ref, st_ref, xw_ref, dec_ref) = refs
    c = pl.program_id(1)

    @pl.when(c == 0)
    def _():
        st_ref[...] = init_ref[...] if has_init else jnp.zeros_like(st_ref)

    a_row = -jnp.exp(alr_ref[...])
    dirs = ((xf_ref, btf_ref, dtf_ref, entf_ref), (xb_ref, btb_ref, dtb_ref, entb_ref))
    order = [[(s, s) for s in range(CPS)], [(s, CPS - 1 - s) for s in range(CPS)]]
    for d, (x_ref, bt_ref, dt_ref, ent_ref) in enumerate(dirs):
        for s, ck in order[d]:
            rows = slice(ck * Q, (ck + 1) * Q)
            dt = dt_ref[rows, :]
            acum = _dot3_rhs(_tri("le" if d == 0 else "ge").astype(BF), dt * a_row)
            tot = acum[Q - 1:Q] if d == 0 else acum[0:1]
            ex = _dot(_expand_lhs(dt * jnp.exp(tot - acum), jnp.exp(tot)), e2_ref[:, d * DI:(d + 1) * DI])
            dec_ref[s, d] = (ex[Q + 2:Q + 3] + ex[Q + 1:Q + 2]) + ex[Q:Q + 1]
            xw_ref[s, d] = x_ref[rows, :] * ex[0:Q].astype(BF)
    for s in range(CPS):
        for d, (x_ref, bt_ref, dt_ref, ent_ref) in enumerate(dirs):
            ck = order[d][s][1]
            for g in range(G):
                cols = slice(g * GW, (g + 1) * GW)
                upd = _dot(bt_ref[g * N:(g + 1) * N, ck * Q:(ck + 1) * Q], xw_ref[s, d, :, cols])
                st = st_ref[d, g]
                ent_ref[ck, g] = st.astype(BF)
                st_ref[d, g] = st * dec_ref[s, d, :, cols] + upd

    @pl.when(c == nc - 1)
    def _():
        fin_ref[...] = st_ref[...]


def _ssd_state(xc, bt, dt, init, alog_r, expand2, l, nb, L):
    CPS = min(MAX_CPS, L // Q)
    nc = L // (CPS * Q)
    fw = lambda b, c: b * nc + c
    bw = lambda b, c: b * nc + (nc - 1 - c)
    has_init = init is not None
    chunk_specs = lambda ch: [
        pl.BlockSpec((CPS * Q, DI), lambda b, c: (ch(b, c), 0)),
        pl.BlockSpec((GN, CPS * Q), lambda b, c: (0, ch(b, c))),
        pl.BlockSpec((CPS * Q, DTW), lambda b, c: (ch(b, c), 0)),
    ]
    st_spec = pl.BlockSpec((None, 2, G, N, GW), lambda b, c: (b, 0, 0, 0, 0))
    in_specs = chunk_specs(fw) + chunk_specs(bw) + ([st_spec] if has_init else []) + [
        _layer_spec((1, DTW), l),
        pl.BlockSpec((2 * DTW, 2 * DI), lambda b, c: (0, 0)),
    ]
    args = (xc, bt, dt, xc, bt, dt) + ((init,) if has_init else ()) + (alog_r, expand2)
    return pl.pallas_call(
        functools.partial(_ssd_state_kernel, nc=nc, has_init=has_init),
        grid=(nb, nc),
        in_specs=in_specs,
        out_specs=[
            pl.BlockSpec((None, CPS, G, N, GW), lambda b, c: (b, c, 0, 0, 0)),
            pl.BlockSpec((None, CPS, G, N, GW), lambda b, c: (b, nc - 1 - c, 0, 0, 0)),
            st_spec,
        ],
        out_shape=[
            jax.ShapeDtypeStruct((nb, nc * CPS, G, N, GW), BF),
            jax.ShapeDtypeStruct((nb, nc * CPS, G, N, GW), BF),
            jax.ShapeDtypeStruct((nb, 2, G, N, GW), F32),
        ],
        scratch_shapes=[pltpu.VMEM((2, G, N, GW), F32), pltpu.VMEM((CPS, 2, Q, DI), BF),
                        pltpu.VMEM((CPS, 2, 1, DI), F32)],
        compiler_params=_params(("arbitrary", "arbitrary")),
        name="ssd_state",
    )(*args)


OUT_CPS = 4


def _fill_blockdiag(bd_ref, xg):
    blk = lax.broadcasted_iota(jnp.int32, (Q, GW), 1) // P
    for r in range(R):
        bd_ref[r * Q:(r + 1) * Q, :] = jnp.where(blk == r, xg, jnp.zeros_like(xg))


def _ssd_out_kernel(xc_ref, bt_ref, dt_ref, dtt_ref, entf_ref, entb_ref,
                    alr_ref, alc_ref, dsk_ref, o_ref, bd_ref):
    x_ref = xc_ref.at[:, 0:DI]
    c_ref = xc_ref.at[:, DI:DI + GN]
    a_row = -jnp.exp(alr_ref[...])
    a_col = -jnp.exp(alc_ref[...])
    left = lax.broadcasted_iota(jnp.int32, (Q, 2 * P), 1) < P
    mask = [_tri("le"), _tri("ge")]
    ri = lax.broadcasted_iota(jnp.int32, (Q, Q), 0)
    ci = lax.broadcasted_iota(jnp.int32, (Q, Q), 1)
    below, above = ci < ri, ci > ri
    ents = (entf_ref, entb_ref)
    dsum = dsk_ref[0:1, :] + dsk_ref[1:2, :]
    for s in range(o_ref.shape[0] // Q):
        rows = slice(s * Q, (s + 1) * Q)
        dt = dt_ref[rows, :]
        dtt = dtt_ref[:, rows]
        log_dtt = jnp.log(dtt)
        rowd = jnp.log(dtt[0:H] + dtt[H:2 * H]) * LOG2E
        acum, rowv = [], []
        for d in range(2):
            acum.append(_dot3_rhs(mask[d].astype(BF), dt * a_row) * LOG2E)
            acum_t = _dot3_lhs(dtt * a_col, mask[1 - d].astype(BF))
            rowv.append((log_dtt - acum_t) * LOG2E)
        for g in range(G):
            cols = slice(g * GW, (g + 1) * GW)
            xg = x_ref[rows, cols]
            cg = c_ref[rows, g * N:(g + 1) * N]
            cbb = _dot(cg, bt_ref[g * N:(g + 1) * N, rows]).astype(BF)
            _fill_blockdiag(bd_ref.at[s, g], xg)
            yin = [_dot(cg, ents[d][s, g]) for d in range(2)]
            ms, yi = [], []
            for r0 in range(0, R, 2):
                es = ([], [])
                for h in (g * R + r0, g * R + r0 + 1):
                    acols = [jnp.broadcast_to(acum[d][:, d * H + h:d * H + h + 1], (Q, Q)) for d in range(2)]
                    expo = jnp.where(below, acols[0] + rowv[0][h:h + 1, :],
                                     jnp.where(above, acols[1] + rowv[1][H + h:H + h + 1, :], rowd[h:h + 1, :]))
                    ms.append(cbb * jnp.exp2(expo).astype(BF))
                    for d in range(2):
                        es[d].append(jnp.exp2(acols[d]))
                pc = slice(r0 * P, (r0 + 2) * P)
                yi.append(yin[0][:, pc] * jnp.where(left, es[0][0], es[0][1])
                          + yin[1][:, pc] * jnp.where(left, es[1][0], es[1][1]))
            y = dsum[:, cols] * xg.astype(F32) + _dot(jnp.concatenate(ms, axis=1), bd_ref[s, g])
            o_ref[rows, cols] = (y + jnp.concatenate(yi, axis=1)).astype(BF)


def _ssd_out(xc, bt, dt, dtt, entf, entb, alog_r, alog_c, dskip_x, l, nb, L):
    cps = min(OUT_CPS, L // Q)
    rows = cps * Q
    nc = L // rows
    T = nb * L
    ch = lambda b, c: b * nc + c
    ent_spec = pl.BlockSpec((None, cps, G, N, GW), lambda b, c: (b, c, 0, 0, 0))
    return pl.pallas_call(
        _ssd_out_kernel,
        grid=(nb, nc),
        in_specs=[
            pl.BlockSpec((rows, DI + GN), lambda b, c: (ch(b, c), 0)),
            pl.BlockSpec((GN, rows), lambda b, c: (0, ch(b, c))),
            pl.BlockSpec((rows, DTW), lambda b, c: (ch(b, c), 0)),
            pl.BlockSpec((DTW, rows), lambda b, c: (0, ch(b, c))),
            ent_spec,
            ent_spec,
            _layer_spec((1, DTW), l),
            _layer_spec((DTW, 1), l),
            _layer_spec((2, DI), l),
        ],
        out_specs=pl.BlockSpec((rows, DI), lambda b, c: (ch(b, c), 0)),
        out_shape=jax.ShapeDtypeStruct((T, DI), BF),
        scratch_shapes=[pltpu.VMEM((cps, G, R * Q, GW), BF)],
        compiler_params=_params(("parallel", "parallel")),
        name="ssd_out",
    )(xc, bt, dt, dtt, entf, entb, alog_r, alog_c, dskip_x)


TM_MIX = 256
MIX_SUB = 4


def _mixer_kernel(y_ref, z_ref, pgg_ref, x_ref, ga_ref, nw_ref, band_ref, cnt_ref, wso_ref,
                  pw_ref, psc_ref, wpo_ref, wo_ref, o_ref):
    u_ref = pgg_ref.at[:, 0:D]
    gs_ref = pgg_ref.at[:, C_GS - C_POOL:C_GS - C_POOL + D]
    gp_ref = pgg_ref.at[:, C_GP - C_POOL:C_GP - C_POOL + D]
    for s in range(MIX_SUB):
        rows = slice(s * TM_MIX, (s + 1) * TM_MIX)
        yns = []
        for g in range(G):
            cols = slice(g * GW, (g + 1) * GW)
            yz = (y_ref[rows, cols] * z_ref[rows, cols]).astype(F32)
            yn = yz * lax.rsqrt(jnp.mean(yz * yz, axis=-1, keepdims=True) + EPS)
            yns.append((yn * nw_ref[:, cols]).astype(BF))
        o_ssd = _dot(jnp.concatenate(yns, axis=1), wso_ref[...])
        pms = []
        for gi in range(NPW):
            ug = u_ref[rows, gi * PG:(gi + 1) * PG]
            wsum = _dot(band_ref[gi], ug)
            pm = wsum / cnt_ref[:, gi:gi + 1] - ug.astype(F32)
            pms.append(_dot(pm.astype(BF), pw_ref[gi]))
        pmc = (jnp.concatenate(pms, axis=1) * psc_ref[...]).astype(BF)
        o_pool = _dot(pmc, wpo_ref[...])
        mix = gs_ref[rows, :].astype(F32) * o_ssd + gp_ref[rows, :].astype(F32) * o_pool
        o_ref[rows, :] = x_ref[rows, :] + ga_ref[...] * _dot(mix.astype(BF), wo_ref[...])


def _pool_tables(seg):
    t = np.arange(TM_MIX)
    same = (t[:, None] // seg) == (t[None, :] // seg)
    ts = t % seg
    bands, cnts = [], []
    for k in POOL_WINDOWS:
        lo = t[:, None] - k // 2
        hi = t[:, None] + k // 2
        bands.append(same & (t[None, :] >= lo) & (t[None, :] < hi))
        cnts.append(np.minimum(ts + k // 2, seg) - np.maximum(ts - k // 2, 0))
    cnt = np.ones((TM_MIX, LANES), np.float32)
    cnt[:, :NPW] = np.stack(cnts, axis=1)
    return jnp.asarray(np.stack(bands).astype(np.float32), BF), jnp.asarray(cnt)


def _mixer_out(y, main, x2, mod3, l, row_fn, seg, norm_w, w_ssd_out, pool_w, pool_scale, w_pool_out, w_out):
    T = x2.shape[0]
    tm = TM_MIX * MIX_SUB
    band, cnt = _pool_tables(seg)
    return pl.pallas_call(
        _mixer_kernel,
        grid=(T // tm,),
        in_specs=[
            pl.BlockSpec((tm, DI), lambda i: (i, 0)),
            pl.BlockSpec((tm, DI), lambda i: (i, C_Z // DI)),
            pl.BlockSpec((tm, MAIN_W - C_POOL), lambda i: (i, C_POOL // (MAIN_W - C_POOL))),
            pl.BlockSpec((tm, D), lambda i: (i, 0)),
            _mod_spec(l, 2, row_fn),
            _layer_spec((1, DI), l),
            pl.BlockSpec((NPW, TM_MIX, TM_MIX), lambda i: (0, 0, 0)),
            pl.BlockSpec((TM_MIX, LANES), lambda i: (0, 0)),
            _layer_spec((DI, D), l),
            _layer_spec((NPW, PG, PG), l),
            _layer_spec((1, D), l),
            _layer_spec((D, D), l),
            _layer_spec((D, D), l),
        ],
        out_specs=pl.BlockSpec((tm, D), lambda i: (i, 0)),
        out_shape=jax.ShapeDtypeStruct((T, D), F32),
        compiler_params=_params(("parallel",)),
        name="mixer_out",
    )(y, main, main, x2, mod3, norm_w, band, cnt, w_ssd_out, pool_w, pool_scale, w_pool_out, w_out)


FFN_CHUNKS = (768, 768, 768, 512)


def _ffn_kernel(x_ref, xn_ref, g_ref, sh_ref, sc_ref, ga_ref, shn_ref, scn_ref, wg_ref, wu_ref, wd_ref, gf_ref,
                o_ref, h_ref, *, final):
    i = pl.program_id(0)
    slot = i % 2

    def norm_mod(x, sh, sc):
        y = x * lax.rsqrt(jnp.mean(x * x, axis=-1, keepdims=True) + EPS)
        return ((y * g_ref[...]) * (1.0 + sc) + sh).astype(BF)

    @pl.when(i == 0)
    def _():
        h_ref[0] = norm_mod(x_ref[...], sh_ref[...], sc_ref[...])

    h_ref[1 - slot] = norm_mod(xn_ref[...], shn_ref[...], scn_ref[...])
    h = h_ref[slot]
    acc = None
    c0 = 0
    for cw in FFN_CHUNKS:
        act = (_silu(_dot(h, wg_ref[:, c0:c0 + cw])) * _dot(h, wu_ref[:, c0:c0 + cw])).astype(BF)
        part = _dot(act, wd_ref[c0:c0 + cw, :])
        acc = part if acc is None else acc + part
        c0 += cw
    xn = x_ref[...] + ga_ref[...] * acc
    if final:
        xn = (xn * lax.rsqrt(jnp.mean(xn * xn, axis=-1, keepdims=True) + EPS)) * gf_ref[...]
    o_ref[...] = xn


def _ffn(x2, g, mod3, l, row_fn, w_gate_up, w_down, g_final, tm, final):
    assert sum(FFN_CHUNKS) == FH
    T = x2.shape[0]
    n = T // tm
    nxt = lambda i: jnp.minimum(i + 1, n - 1)
    once = pl.Buffered(1)
    return pl.pallas_call(
        functools.partial(_ffn_kernel, final=final),
        grid=(n,),
        in_specs=[
            pl.BlockSpec((tm, D), lambda i: (i, 0)),
            pl.BlockSpec((tm, D), lambda i: (nxt(i), 0)),
            _layer_spec((1, D), l),
            _mod_spec(l, 3, row_fn),
            _mod_spec(l, 4, row_fn),
            _mod_spec(l, 5, row_fn),
            _mod_spec(l, 3, lambda i: row_fn(nxt(i))),
            _mod_spec(l, 4, lambda i: row_fn(nxt(i))),
            pl.BlockSpec((None, D, FH), lambda i: (l, 0, 0), pipeline_mode=once),
            pl.BlockSpec((None, D, FH), lambda i: (l, 0, 1), pipeline_mode=once),
            pl.BlockSpec((None, FH, D), lambda i: (l, 0, 0), pipeline_mode=once),
            pl.BlockSpec((1, D), lambda i: (0, 0)),
        ],
        out_specs=pl.BlockSpec((tm, D), lambda i: (i, 0)),
        out_shape=jax.ShapeDtypeStruct((T, D), F32),
        scratch_shapes=[pltpu.VMEM((2, tm, D), BF)],
        compiler_params=_params(("arbitrary",)),
        name="ffn",
    )(x2, x2, g, mod3, mod3, mod3, mod3, mod3, w_gate_up, w_gate_up, w_down, g_final)


def kernel(x, c, ctx, c_ctx, w_ada, b_ada, g_mix, w_in, conv_w, conv_b, dt_bias, a_log, d_skip, ssd_norm_w,
           w_ssd_out, pool_w, pool_scale, w_pool_out, w_out, g_ffn, w_gate_up, w_down, g_final):
    nb, L, _ = x.shape
    Lc = ctx.shape[1]
    ctx_row = nb

    cond8 = jnp.zeros((8, D), F32).at[:nb].set(c).at[ctx_row].set(c_ctx)
    mod3 = _adaln(cond8, w_ada, b_ada).reshape(DEPTH * 8 * 6, 1, D)

    shift = jnp.asarray(_shift_table(), BF)
    expand2 = jnp.asarray(_expand_table(), BF)

    w_in_t = jnp.swapaxes(w_in, 1, 2)
    w_main = w_in_t.astype(BF)
    w_dt = jnp.pad(w_in_t[:, C_POOL:C_POOL + 2 * H], ((0, 0), (0, DTW - 2 * H), (0, 0)))
    pad_heads = lambda a: jnp.pad(a.reshape(DEPTH, 1, 2 * H), ((0, 0), (0, 0), (0, DTW - 2 * H)))
    dtb = pad_heads(dt_bias)
    alog_r = pad_heads(a_log)
    alog_c = alog_r.reshape(DEPTH, DTW, 1)
    dskip_x = jnp.repeat(d_skip, P, axis=2)
    norm_w = ssd_norm_w.reshape(DEPTH, 1, DI)
    gm = g_mix.reshape(DEPTH, 1, D)
    gf = g_ffn.reshape(DEPTH, 1, D)
    cbias = conv_b.reshape(DEPTH, 1, -1)
    wso = w_ssd_out.astype(BF)
    pw = pool_w.astype(BF)
    psc = pool_scale.reshape(DEPTH, 1, D)
    wpo = w_pool_out.astype(BF)
    wo = w_out.astype(BF)
    wgu = w_gate_up.astype(BF)
    wdn = w_down.astype(BF)
    gfin = g_final.reshape(1, D)

    xl = x.reshape(nb * L, D)
    xc_ = ctx.reshape(nb * Lc, D)
    tm_l = IN_TM
    tm_f = FFN_TM
    for l in range(DEPTH):
        last = l == DEPTH - 1

        def prepare(x2, row_fn, tm, seq):
            main, dt, dtt = _in_proj(x2, gm, mod3, l, row_fn, w_main, w_dt, dtb, tm)
            xcv = _conv(main, shift, conv_w, cbias, l, nb, seq, transpose=False)
            bt = _conv(main, shift, conv_w, cbias, l, nb, seq, transpose=True)
            return main, dt, dtt, xcv, bt

        def mix_and_ffn(x2, parts, ent, row_fn_mix, row_fn_ffn, seq, seg, final):
            main, dt, dtt, xcv, bt = parts
            y = _ssd_out(xcv, bt, dt, dtt, ent[0], ent[1], alog_r, alog_c, dskip_x, l, nb, seq)
            x2 = _mixer_out(y, main, x2, mod3, l, row_fn_mix, seg, norm_w, wso, pw, psc, wpo, wo)
            return _ffn(x2, gf, mod3, l, row_fn_ffn, wgu, wdn, gfin, tm_f, final)

        cparts = prepare(xc_, lambda i: ctx_row, min(IN_TM, nb * Lc), Lc)
        centf, centb, cfin = _ssd_state(cparts[3], cparts[4], cparts[1], None, alog_r, expand2, l, nb, Lc)
        lparts = prepare(xl, lambda i: i // (L // tm_l), tm_l, L)
        lentf, lentb, _ = _ssd_state(lparts[3], lparts[4], lparts[1], cfin, alog_r, expand2, l, nb, L)
        xl = mix_and_ffn(xl, lparts, (lentf, lentb), lambda i: i // (L // (TM_MIX * MIX_SUB)),
                         lambda i: i // (L // tm_f), L, GRID_W, last)
        if not last:
            xc_ = mix_and_ffn(xc_, cparts, (centf, centb), lambda i: ctx_row, lambda i: ctx_row, Lc, Lc, False)
    return xl.reshape(nb, L, D)
```

```python
import functools

import numpy as np
import jax
import jax.numpy as jnp
from jax import lax
from jax.experimental import pallas as pl
from jax.experimental.pallas import tpu as pltpu

F32 = jnp.float32
BF = jnp.bfloat16

D = 1024
DEPTH = 2
H = 32
P = 64
G = 8
R = H // G
N = 128
Q = 128
DI = H * P
GN = G * N
GW = R * P
KC = 5
FH = 2816
POOL_WINDOWS = (2, 4, 8, 16)
NPW = len(POOL_WINDOWS)
PG = D // NPW
GRID_W = 64
EPS = 1e-6
LOG2E = 1.4426950408889634
DTW = 128

C_Z, C_X, C_B, C_C, C_POOL, C_GS, C_GP = 0, 2048, 4096, 5120, 6144, 7168, 8192
MAIN_W = 9216

LANES = 128
VMEM_LIMIT = 56 * 1024 * 1024

ADALN_TN = 1536
IN_TM, IN_TN = 1024, 3072
CONV_TL, CONV_TC = 1024, 1024
FFN_TM = 512


def _dot(a, b):
    return jnp.dot(a, b, preferred_element_type=F32)


def _dot_nt(a, b):
    return lax.dot_general(a, b, (((1,), (1,)), ((), ())), preferred_element_type=F32)


def _split2(a):
    hi = a.astype(BF)
    return hi, (a - hi.astype(F32)).astype(BF)


def _split3(a):
    hi = a.astype(BF)
    r = a - hi.astype(F32)
    mid = r.astype(BF)
    lo = (r - mid.astype(F32)).astype(BF)
    return hi, mid, lo


def _dot3_rhs(m, a):
    hi, mid, lo = _split3(a)
    return (_dot(m, lo) + _dot(m, mid)) + _dot(m, hi)


def _dot3_lhs(a, m):
    hi, mid, lo = _split3(a)
    return (_dot(lo, m) + _dot(mid, m)) + _dot(hi, m)


def _sigmoid(v):
    return 0.5 * jnp.tanh(0.5 * v) + 0.5


def _silu(v):
    hv = 0.5 * v
    return hv * jnp.tanh(hv) + hv


def _params(sem):
    return pltpu.CompilerParams(dimension_semantics=sem, vmem_limit_bytes=VMEM_LIMIT)


def _mod_spec(l, k, row_fn):
    return pl.BlockSpec((None, 1, D), lambda *ids: ((l * 8 + row_fn(*ids)) * 6 + k, 0, 0))


def _layer_spec(shape, l):
    zeros = (0,) * len(shape)
    return pl.BlockSpec((None,) + tuple(shape), lambda *ids: (l,) + zeros)


def _adaln_kernel(c_ref, w_ref, b_ref, o_ref):
    s_hi, s_lo = _split2(_silu(c_ref[...]))
    w_hi, w_lo = _split2(w_ref[...])
    acc = (_dot(s_lo, w_hi) + _dot(s_hi, w_lo)) + _dot(s_hi, w_hi)
    o_ref[...] = acc + b_ref[...]


def _adaln(cond8, w_ada, b_ada):
    tn = ADALN_TN
    return pl.pallas_call(
        _adaln_kernel,
        grid=(DEPTH, 6 * D // tn),
        in_specs=[
            pl.BlockSpec((8, D), lambda l, j: (0, 0)),
            pl.BlockSpec((None, D, tn), lambda l, j: (l, 0, j)),
            pl.BlockSpec((None, 1, tn), lambda l, j: (l, 0, j)),
        ],
        out_specs=pl.BlockSpec((None, 8, tn), lambda l, j: (l, 0, j)),
        out_shape=jax.ShapeDtypeStruct((DEPTH, 8, 6 * D), F32),
        compiler_params=_params(("arbitrary", "arbitrary")),
        name="adaln",
    )(cond8, w_ada, b_ada.reshape(DEPTH, 1, 6 * D))


def _inproj_kernel(x_ref, xn_ref, g_ref, sh_ref, sc_ref, shn_ref, scn_ref, w_ref, wdt_ref, dtb_ref,
                   o_ref, dt_ref, dtt_ref, h_ref, dts_ref, *, slab, nj):
    i = pl.program_id(0)
    j = pl.program_id(1)
    slot = i % 2
    tm, tn = o_ref.shape
    wh, wl = _split2(wdt_ref[...])
    wcat = jnp.concatenate([wh, wl], axis=0)

    def prologue(x, sh, sc):
        y = x * lax.rsqrt(jnp.mean(x * x, axis=-1, keepdims=True) + EPS)
        h = (y * g_ref[...]) * (1.0 + sc) + sh
        hb, hl = _split2(h)
        p = _dot_nt(hb, wcat)
        d = (_dot_nt(hl, wh) + p[:, DTW:]) + p[:, :DTW] + dtb_ref[...]
        return hb, jnp.maximum(d, 0.0) + jnp.log1p(jnp.exp(-jnp.abs(d)))

    @pl.when((i == 0) & (j == 0))
    def _():
        hb, dt = prologue(x_ref[...], sh_ref[...], sc_ref[...])
        h_ref[0] = hb
        dts_ref[0] = dt

    @pl.when(j == 0)
    def _():
        dt = dts_ref[slot]
        dt_ref[...] = dt
        dtt_ref[...] = dt.T

    def step(jj):
        rows = slice(min(jj * slab, tm - slab), min(jj * slab, tm - slab) + slab)
        hb, dt = prologue(xn_ref[rows, :], shn_ref[...], scn_ref[...])
        h_ref[1 - slot, rows, :] = hb
        dts_ref[1 - slot, rows, :] = dt
        v = _dot_nt(h_ref[slot], w_ref[0])
        lo, hi = jj * tn, (jj + 1) * tn
        n_silu = min(max(C_X - lo, 0), tn)
        n_sig = min(max(hi - C_GS, 0), tn)
        if n_silu:
            o_ref[:, :n_silu] = _silu(v[:, :n_silu]).astype(BF)
        if tn - n_silu - n_sig:
            o_ref[:, n_silu:tn - n_sig] = v[:, n_silu:tn - n_sig].astype(BF)
        if n_sig:
            o_ref[:, tn - n_sig:] = _sigmoid(v[:, tn - n_sig:]).astype(BF)

    for jj in range(nj):
        pl.when(j == jj)(functools.partial(step, jj))


def _in_proj(x2, g, mod3, l, row_fn, w_main, w_dt, dt_bias, tm):
    T = x2.shape[0]
    n = T // tm
    tn = IN_TN
    nj = MAIN_W // tn
    slab = -(-tm // (nj * Q)) * Q
    nlow = C_POOL // tn
    w_row = lambda j: pl.multiple_of(jnp.where(j < nlow, j * tn, j * tn + 2 * H), 2 * H)
    nxt = lambda i: jnp.minimum(i + 1, n - 1)
    return pl.pallas_call(
        functools.partial(_inproj_kernel, slab=slab, nj=nj),
        grid=(n, nj),
        in_specs=[
            pl.BlockSpec((tm, D), lambda i, j: (0, 0)),
            pl.BlockSpec((tm, D), lambda i, j: (nxt(i), 0)),
            _layer_spec((1, D), l),
            _mod_spec(l, 0, lambda i, j: row_fn(0)),
            _mod_spec(l, 1, lambda i, j: row_fn(0)),
            _mod_spec(l, 0, lambda i, j: row_fn(nxt(i))),
            _mod_spec(l, 1, lambda i, j: row_fn(nxt(i))),
            pl.BlockSpec((pl.Element(1), pl.Element(tn), pl.Element(D)), lambda i, j: (l, w_row(j), 0)),
            _layer_spec((DTW, D), l),
            _layer_spec((1, DTW), l),
        ],
        out_specs=[
            pl.BlockSpec((tm, tn), lambda i, j: (i, j)),
            pl.BlockSpec((tm, DTW), lambda i, j: (i, 0)),
            pl.BlockSpec((DTW, tm), lambda i, j: (0, i)),
        ],
        out_shape=[
            jax.ShapeDtypeStruct((T, MAIN_W), BF),
            jax.ShapeDtypeStruct((T, DTW), F32),
            jax.ShapeDtypeStruct((DTW, T), F32),
        ],
        scratch_shapes=[pltpu.VMEM((2, tm, D), BF), pltpu.VMEM((2, tm, DTW), F32)],
        compiler_params=_params(("arbitrary", "arbitrary")),
        name="in_proj",
    )(x2, x2, g, mod3, mod3, mod3, mod3, w_main, w_dt, dt_bias)


CH = 64
SUBL = 8
CONV_SUB = 256


def _shift_table():
    s = np.zeros((Q // SUBL, KC - 1, SUBL, 2 * Q), np.float32)
    t = np.arange(Q)
    for si, k in enumerate([k for k in range(KC) if k != KC // 2]):
        s[t // SUBL, si, t % SUBL, CH + t + k - KC // 2] = 1.0
    return s.reshape((KC - 1) * Q, 2 * Q)


def _conv_kernel(prev_ref, cur_ref, next_ref, s_ref, w_ref, b_ref, o_ref, ext_ref, *, tl, nl, transpose):
    i = pl.program_id(1)
    prev = prev_ref[...]
    nxt = next_ref[...]
    ext_ref[0:CH, :] = jnp.where(i > 0, prev, jnp.zeros_like(prev))
    ext_ref[CH:CH + tl, :] = cur_ref[...]
    ext_ref[CH + tl:2 * CH + tl, :] = jnp.where(i < nl - 1, nxt, jnp.zeros_like(nxt))
    tc = cur_ref.shape[1]
    for r in range(tl // Q):
        for c0 in range(0, tc, CONV_SUB):
            cols = slice(c0, c0 + CONV_SUB)
            win = ext_ref[r * Q:(r + 2) * Q, cols]
            sh = _dot(s_ref[...], win)
            hv = 0.5 * b_ref[:, cols]
            si = 0
            for k in range(KC):
                if k == KC // 2:
                    tap = win[CH:CH + Q, :].astype(F32)
                else:
                    starts = [(tg * (KC - 1) + si) * SUBL for tg in range(Q // SUBL)]
                    tap = jnp.concatenate([sh[a:a + SUBL, :] for a in starts], axis=0)
                    si += 1
                hv = hv + (0.5 * w_ref[k:k + 1, cols]) * tap
            out = hv * jnp.tanh(hv) + hv
            if transpose:
                o_ref[cols, r * Q:(r + 1) * Q] = out.T.astype(BF)
            else:
                o_ref[r * Q:(r + 1) * Q, cols] = out.astype(BF)


def _conv(main, shift, conv_w, conv_b, l, nb, L, transpose):
    T = nb * L
    tc = CONV_TC
    tl = min(L, CONV_TL)
    nl = L // tl
    rb = tl // CH
    last = T // CH - 1
    if transpose:
        ncol = GN // tc
        in_cb = lambda j: j + C_B // tc
        w_cb = lambda j: j + (C_B - C_X) // tc
        out_spec = pl.BlockSpec((tc, tl), lambda b, i, j: (j, b * nl + i))
        out_shape = jax.ShapeDtypeStruct((GN, T), BF)
    else:
        nx = DI // tc
        ncol = (DI + GN) // tc
        in_cb = lambda j: jnp.where(j < nx, j + C_X // tc, j - nx + C_C // tc)
        w_cb = lambda j: jnp.where(j < nx, j, j - nx + (C_C - C_X) // tc)
        out_spec = pl.BlockSpec((tl, tc), lambda b, i, j: (b * nl + i, j))
        out_shape = jax.ShapeDtypeStruct((T, DI + GN), BF)
    return pl.pallas_call(
        functools.partial(_conv_kernel, tl=tl, nl=nl, transpose=transpose),
        grid=(nb, nl, ncol),
        in_specs=[
            pl.BlockSpec((CH, tc), lambda b, i, j: (jnp.maximum((b * nl + i) * rb - 1, 0), in_cb(j))),
            pl.BlockSpec((tl, tc), lambda b, i, j: (b * nl + i, in_cb(j))),
            pl.BlockSpec((CH, tc), lambda b, i, j: (jnp.minimum((b * nl + i + 1) * rb, last), in_cb(j))),
            pl.BlockSpec(((KC - 1) * Q, 2 * Q), lambda b, i, j: (0, 0)),
            pl.BlockSpec((None, KC, tc), lambda b, i, j: (l, 0, w_cb(j))),
            pl.BlockSpec((None, 1, tc), lambda b, i, j: (l, 0, w_cb(j))),
        ],
        out_specs=out_spec,
        out_shape=out_shape,
        scratch_shapes=[pltpu.VMEM((tl + 2 * CH, tc), BF)],
        compiler_params=_params(("arbitrary", "arbitrary", "arbitrary")),
        name="conv_t" if transpose else "conv",
    )(main, main, main, shift, conv_w, conv_b)


def _tri(kind):
    r = lax.broadcasted_iota(jnp.int32, (Q, Q), 0)
    c = lax.broadcasted_iota(jnp.int32, (Q, Q), 1)
    return (c <= r) if kind == "le" else (c >= r)


def _expand_table():
    e = np.zeros((DTW, 2 * DI), np.float32)
    for k in range(2 * H):
        e[k, k * P:(k + 1) * P] = 1.0
    return np.concatenate([e, e], axis=0)


XROWS = 16


def _expand_lhs(w, dec):
    w_hi, w_lo = _split2(w)
    d_hi, d_mid, d_lo = [t.astype(F32) for t in _split3(jnp.broadcast_to(dec, (XROWS, DTW)))]
    rid = lax.broadcasted_iota(jnp.int32, (XROWS, DTW), 0)
    extra = jnp.where(rid == 0, d_hi, jnp.where(rid == 1, d_mid, jnp.where(rid == 2, d_lo, 0.0))).astype(BF)
    zero = jnp.zeros((XROWS, DTW), BF)
    return jnp.concatenate([jnp.concatenate([w_hi, w_lo], axis=1), jnp.concatenate([extra, zero], axis=1)], axis=0)


MAX_CPS = 4


def _ssd_state_kernel(*refs, nc, has_init):
    CPS = refs[-1].shape[0]
    if has_init:
        (xf_ref, btf_ref, dtf_ref, xb_ref, btb_ref, dtb_ref, init_ref, alr_ref, e2_ref,
         entf_ref, entb_ref, fin_ref, st_ref, xw_ref, dec_ref) = refs
    else:
        (xf_ref, btf_ref, dtf_ref, xb_ref, btb_ref, dtb_ref, alr_ref, e2_ref,
         entf_ref, entb_ref, fin_ref, st_ref, xw_ref, dec_ref) = refs
    c = pl.program_id(1)

    @pl.when(c == 0)
    def _():
        st_ref[...] = init_ref[...] if has_init else jnp.zeros_like(st_ref)

    a_row = -jnp.exp(alr_ref[...])
    dirs = ((xf_ref, btf_ref, dtf_ref, entf_ref), (xb_ref, btb_ref, dtb_ref, entb_ref))
    order = [[(s, s) for s in range(CPS)], [(s, CPS - 1 - s) for s in range(CPS)]]
    for d, (x_ref, bt_ref, dt_ref, ent_ref) in enumerate(dirs):
        for s, ck in order[d]:
            rows = slice(ck * Q, (ck + 1) * Q)
            dt = dt_ref[rows, :]
            acum = _dot3_rhs(_tri("le" if d == 0 else "ge").astype(BF), dt * a_row)
            tot = acum[Q - 1:Q] if d == 0 else acum[0:1]
            ex = _dot(_expand_lhs(dt * jnp.exp(tot - acum), jnp.exp(tot)), e2_ref[:, d * DI:(d + 1) * DI])
            dec_ref[s, d] = (ex[Q + 2:Q + 3] + ex[Q + 1:Q + 2]) + ex[Q:Q + 1]
            xw_ref[s, d] = x_ref[rows, :] * ex[0:Q].astype(BF)
    for s in range(CPS):
        for d, (x_ref, bt_ref, dt_ref, ent_ref) in enumerate(dirs):
            ck = order[d][s][1]
            for g in range(G):
                cols = slice(g * GW, (g + 1) * GW)
                upd = _dot(bt_ref[g * N:(g + 1) * N, ck * Q:(ck + 1) * Q], xw_ref[s, d, :, cols])
                st = st_ref[d, g]
                ent_ref[ck, g] = st.astype(BF)
                st_ref[d, g] = st * dec_ref[s, d, :, cols] + upd

    @pl.when(c == nc - 1)
    def _():
        fin_ref[...] = st_ref[...]


def _ssd_state(xc, bt, dt, init, alog_r, expand2, l, nb, L):
    CPS = min(MAX_CPS, L // Q)
    nc = L // (CPS * Q)
    fw = lambda b, c: b * nc + c
    bw = lambda b, c: b * nc + (nc - 1 - c)
    has_init = init is not None
    chunk_specs = lambda ch: [
        pl.BlockSpec((CPS * Q, DI), lambda b, c: (ch(b, c), 0)),
        pl.BlockSpec((GN, CPS * Q), lambda b, c: (0, ch(b, c))),
        pl.BlockSpec((CPS * Q, DTW), lambda b, c: (ch(b, c), 0)),
    ]
    st_spec = pl.BlockSpec((None, 2, G, N, GW), lambda b, c: (b, 0, 0, 0, 0))
    in_specs = chunk_specs(fw) + chunk_specs(bw) + ([st_spec] if has_init else []) + [
        _layer_spec((1, DTW), l),
        pl.BlockSpec((2 * DTW, 2 * DI), lambda b, c: (0, 0)),
    ]
    args = (xc, bt, dt, xc, bt, dt) + ((init,) if has_init else ()) + (alog_r, expand2)
    return pl.pallas_call(
        functools.partial(_ssd_state_kernel, nc=nc, has_init=has_init),
        grid=(nb, nc),
        in_specs=in_specs,
        out_specs=[
            pl.BlockSpec((None, CPS, G, N, GW), lambda b, c: (b, c, 0, 0, 0)),
            pl.BlockSpec((None, CPS, G, N, GW), lambda b, c: (b, nc - 1 - c, 0, 0, 0)),
            st_spec,
        ],
        out_shape=[
            jax.ShapeDtypeStruct((nb, nc * CPS, G, N, GW), BF),
            jax.ShapeDtypeStruct((nb, nc * CPS, G, N, GW), BF),
            jax.ShapeDtypeStruct((nb, 2, G, N, GW), F32),
        ],
        scratch_shapes=[pltpu.VMEM((2, G, N, GW), F32), pltpu.VMEM((CPS, 2, Q, DI), BF),
                        pltpu.VMEM((CPS, 2, 1, DI), F32)],
        compiler_params=_params(("arbitrary", "arbitrary")),
        name="ssd_state",
    )(*args)


OUT_CPS = 4


def _fill_blockdiag(bd_ref, xg):
    blk = lax.broadcasted_iota(jnp.int32, (Q, GW), 1) // P
    for r in range(R):
        bd_ref[r * Q:(r + 1) * Q, :] = jnp.where(blk == r, xg, jnp.zeros_like(xg))


def _ssd_out_kernel(xc_ref, bt_ref, dt_ref, dtt_ref, entf_ref, entb_ref,
                    alr_ref, alc_ref, dsk_ref, o_ref, bd_ref):
    x_ref = xc_ref.at[:, 0:DI]
    c_ref = xc_ref.at[:, DI:DI + GN]
    a_row = -jnp.exp(alr_ref[...])
    a_col = -jnp.exp(alc_ref[...])
    left = lax.broadcasted_iota(jnp.int32, (Q, 2 * P), 1) < P
    mask = [_tri("le"), _tri("ge")]
    ri = lax.broadcasted_iota(jnp.int32, (Q, Q), 0)
    ci = lax.broadcasted_iota(jnp.int32, (Q, Q), 1)
    below, above = ci < ri, ci > ri
    ents = (entf_ref, entb_ref)
    dsum = dsk_ref[0:1, :] + dsk_ref[1:2, :]
    for s in range(o_ref.shape[0] // Q):
        rows = slice(s * Q, (s + 1) * Q)
        dt = dt_ref[rows, :]
        dtt = dtt_ref[:, rows]
        log_dtt = jnp.log(dtt)
        rowd = jnp.log(dtt[0:H] + dtt[H:2 * H]) * LOG2E
        acum, rowv = [], []
        for d in range(2):
            acum.append(_dot3_rhs(mask[d].astype(BF), dt * a_row) * LOG2E)
            acum_t = _dot3_lhs(dtt * a_col, mask[1 - d].astype(BF))
            rowv.append((log_dtt - acum_t) * LOG2E)
        for g in range(G):
            cols = slice(g * GW, (g + 1) * GW)
            xg = x_ref[rows, cols]
            cg = c_ref[rows, g * N:(g + 1) * N]
            cbb = _dot(cg, bt_ref[g * N:(g + 1) * N, rows]).astype(BF)
            _fill_blockdiag(bd_ref.at[s, g], xg)
            yin = [_dot(cg, ents[d][s, g]) for d in range(2)]
            ms, yi = [], []
            for r0 in range(0, R, 2):
                es = ([], [])
                for h in (g * R + r0, g * R + r0 + 1):
                    acols = [jnp.broadcast_to(acum[d][:, d * H + h:d * H + h + 1], (Q, Q)) for d in range(2)]
                    expo = jnp.where(below, acols[0] + rowv[0][h:h + 1, :],
                                     jnp.where(above, acols[1] + rowv[1][H + h:H + h + 1, :], rowd[h:h + 1, :]))
                    ms.append(cbb * jnp.exp2(expo).astype(BF))
                    for d in range(2):
                        es[d].append(jnp.exp2(acols[d]))
                pc = slice(r0 * P, (r0 + 2) * P)
                yi.append(yin[0][:, pc] * jnp.where(left, es[0][0], es[0][1])
                          + yin[1][:, pc] * jnp.where(left, es[1][0], es[1][1]))
            y = dsum[:, cols] * xg.astype(F32) + _dot(jnp.concatenate(ms, axis=1), bd_ref[s, g])
            o_ref[rows, cols] = (y + jnp.concatenate(yi, axis=1)).astype(BF)


def _ssd_out(xc, bt, dt, dtt, entf, entb, alog_r, alog_c, dskip_x, l, nb, L):
    cps = min(OUT_CPS, L // Q)
    rows = cps * Q
    nc = L // rows
    T = nb * L
    ch = lambda b, c: b * nc + c
    ent_spec = pl.BlockSpec((None, cps, G, N, GW), lambda b, c: (b, c, 0, 0, 0))
    return pl.pallas_call(
        _ssd_out_kernel,
        grid=(nb, nc),
        in_specs=[
            pl.BlockSpec((rows, DI + GN), lambda b, c: (ch(b, c), 0)),
            pl.BlockSpec((GN, rows), lambda b, c: (0, ch(b, c))),
            pl.BlockSpec((rows, DTW), lambda b, c: (ch(b, c), 0)),
            pl.BlockSpec((DTW, rows), lambda b, c: (0, ch(b, c))),
            ent_spec,
            ent_spec,
            _layer_spec((1, DTW), l),
            _layer_spec((DTW, 1), l),
            _layer_spec((2, DI), l),
        ],
        out_specs=pl.BlockSpec((rows, DI), lambda b, c: (ch(b, c), 0)),
        out_shape=jax.ShapeDtypeStruct((T, DI), BF),
        scratch_shapes=[pltpu.VMEM((cps, G, R * Q, GW), BF)],
        compiler_params=_params(("parallel", "parallel")),
        name="ssd_out",
    )(xc, bt, dt, dtt, entf, entb, alog_r, alog_c, dskip_x)


TM_MIX = 256
MIX_SUB = 4


def _mixer_kernel(y_ref, z_ref, pgg_ref, x_ref, ga_ref, nw_ref, band_ref, cnt_ref, wso_ref,
                  pw_ref, psc_ref, wpo_ref, wo_ref, o_ref):
    u_ref = pgg_ref.at[:, 0:D]
    gs_ref = pgg_ref.at[:, C_GS - C_POOL:C_GS - C_POOL + D]
    gp_ref = pgg_ref.at[:, C_GP - C_POOL:C_GP - C_POOL + D]
    for s in range(MIX_SUB):
        rows = slice(s * TM_MIX, (s + 1) * TM_MIX)
        yns = []
        for g in range(G):
            cols = slice(g * GW, (g + 1) * GW)
            yz = (y_ref[rows, cols] * z_ref[rows, cols]).astype(F32)
            yn = yz * lax.rsqrt(jnp.mean(yz * yz, axis=-1, keepdims=True) + EPS)
            yns.append((yn * nw_ref[:, cols]).astype(BF))
        o_ssd = _dot(jnp.concatenate(yns, axis=1), wso_ref[...])
        pms = []
        for gi in range(NPW):
            ug = u_ref[rows, gi * PG:(gi + 1) * PG]
            wsum = _dot(band_ref[gi], ug)
            pm = wsum / cnt_ref[:, gi:gi + 1] - ug.astype(F32)
            pms.append(_dot(pm.astype(BF), pw_ref[gi]))
        pmc = (jnp.concatenate(pms, axis=1) * psc_ref[...]).astype(BF)
        o_pool = _dot(pmc, wpo_ref[...])
        mix = gs_ref[rows, :].astype(F32) * o_ssd + gp_ref[rows, :].astype(F32) * o_pool
        o_ref[rows, :] = x_ref[rows, :] + ga_ref[...] * _dot(mix.astype(BF), wo_ref[...])


def _pool_tables(seg):
    t = np.arange(TM_MIX)
    same = (t[:, None] // seg) == (t[None, :] // seg)
    ts = t % seg
    bands, cnts = [], []
    for k in POOL_WINDOWS:
        lo = t[:, None] - k // 2
        hi = t[:, None] + k // 2
        bands.append(same & (t[None, :] >= lo) & (t[None, :] < hi))
        cnts.append(np.minimum(ts + k // 2, seg) - np.maximum(ts - k // 2, 0))
    cnt = np.ones((TM_MIX, LANES), np.float32)
    cnt[:, :NPW] = np.stack(cnts, axis=1)
    return jnp.asarray(np.stack(bands).astype(np.float32), BF), jnp.asarray(cnt)


def _mixer_out(y, main, x2, mod3, l, row_fn, seg, norm_w, w_ssd_out, pool_w, pool_scale, w_pool_out, w_out):
    T = x2.shape[0]
    tm = TM_MIX * MIX_SUB
    band, cnt = _pool_tables(seg)
    return pl.pallas_call(
        _mixer_kernel,
        grid=(T // tm,),
        in_specs=[
            pl.BlockSpec((tm, DI), lambda i: (i, 0)),
            pl.BlockSpec((tm, DI), lambda i: (i, C_Z // DI)),
            pl.BlockSpec((tm, MAIN_W - C_POOL), lambda i: (i, C_POOL // (MAIN_W - C_POOL))),
            pl.BlockSpec((tm, D), lambda i: (i, 0)),
            _mod_spec(l, 2, row_fn),
            _layer_spec((1, DI), l),
            pl.BlockSpec((NPW, TM_MIX, TM_MIX), lambda i: (0, 0, 0)),
            pl.BlockSpec((TM_MIX, LANES), lambda i: (0, 0)),
            _layer_spec((DI, D), l),
            _layer_spec((NPW, PG, PG), l),
            _layer_spec((1, D), l),
            _layer_spec((D, D), l),
            _layer_spec((D, D), l),
        ],
        out_specs=pl.BlockSpec((tm, D), lambda i: (i, 0)),
        out_shape=jax.ShapeDtypeStruct((T, D), F32),
        compiler_params=_params(("parallel",)),
        name="mixer_out",
    )(y, main, main, x2, mod3, norm_w, band, cnt, w_ssd_out, pool_w, pool_scale, w_pool_out, w_out)


FFN_CHUNKS = (768, 768, 768, 512)


def _ffn_kernel(x_ref, xn_ref, g_ref, sh_ref, sc_ref, ga_ref, shn_ref, scn_ref, wg_ref, wu_ref, wd_ref, gf_ref,
                o_ref, h_ref, *, final):
    i = pl.program_id(0)
    slot = i % 2

    def norm_mod(x, sh, sc):
        y = x * lax.rsqrt(jnp.mean(x * x, axis=-1, keepdims=True) + EPS)
        return ((y * g_ref[...]) * (1.0 + sc) + sh).astype(BF)

    @pl.when(i == 0)
    def _():
        h_ref[0] = norm_mod(x_ref[...], sh_ref[...], sc_ref[...])

    h_ref[1 - slot] = norm_mod(xn_ref[...], shn_ref[...], scn_ref[...])
    h = h_ref[slot]
    acc = None
    c0 = 0
    for cw in FFN_CHUNKS:
        act = (_silu(_dot(h, wg_ref[:, c0:c0 + cw])) * _dot(h, wu_ref[:, c0:c0 + cw])).astype(BF)
        part = _dot(act, wd_ref[c0:c0 + cw, :])
        acc = part if acc is None else acc + part
        c0 += cw
    xn = x_ref[...] + ga_ref[...] * acc
    if final:
        xn = (xn * lax.rsqrt(jnp.mean(xn * xn, axis=-1, keepdims=True) + EPS)) * gf_ref[...]
    o_ref[...] = xn


def _ffn(x2, g, mod3, l, row_fn, w_gate_up, w_down, g_final, tm, final):
    assert sum(FFN_CHUNKS) == FH
    T = x2.shape[0]
    n = T // tm
    nxt = lambda i: jnp.minimum(i + 1, n - 1)
    once = pl.Buffered(1)
    return pl.pallas_call(
        functools.partial(_ffn_kernel, final=final),
        grid=(n,),
        in_specs=[
            pl.BlockSpec((tm, D), lambda i: (i, 0)),
            pl.BlockSpec((tm, D), lambda i: (nxt(i), 0)),
            _layer_spec((1, D), l),
            _mod_spec(l, 3, row_fn),
            _mod_spec(l, 4, row_fn),
            _mod_spec(l, 5, row_fn),
            _mod_spec(l, 3, lambda i: row_fn(nxt(i))),
            _mod_spec(l, 4, lambda i: row_fn(nxt(i))),
            pl.BlockSpec((None, D, FH), lambda i: (l, 0, 0), pipeline_mode=once),
            pl.BlockSpec((None, D, FH), lambda i: (l, 0, 1), pipeline_mode=once),
            pl.BlockSpec((None, FH, D), lambda i: (l, 0, 0), pipeline_mode=once),
            pl.BlockSpec((1, D), lambda i: (0, 0)),
        ],
        out_specs=pl.BlockSpec((tm, D), lambda i: (i, 0)),
        out_shape=jax.ShapeDtypeStruct((T, D), F32),
        scratch_shapes=[pltpu.VMEM((2, tm, D), BF)],
        compiler_params=_params(("arbitrary",)),
        name="ffn",
    )(x2, x2, g, mod3, mod3, mod3, mod3, mod3, w_gate_up, w_gate_up, w_down, g_final)


def kernel(x, c, ctx, c_ctx, w_ada, b_ada, g_mix, w_in, conv_w, conv_b, dt_bias, a_log, d_skip, ssd_norm_w,
           w_ssd_out, pool_w, pool_scale, w_pool_out, w_out, g_ffn, w_gate_up, w_down, g_final):
    nb, L, _ = x.shape
    Lc = ctx.shape[1]
    ctx_row = nb

    cond8 = jnp.zeros((8, D), F32).at[:nb].set(c).at[ctx_row].set(c_ctx)
    mod3 = _adaln(cond8, w_ada, b_ada).reshape(DEPTH * 8 * 6, 1, D)

    shift = jnp.asarray(_shift_table(), BF)
    expand2 = jnp.asarray(_expand_table(), BF)

    w_in_t = jnp.swapaxes(w_in, 1, 2)
    w_main = w_in_t.astype(BF)
    w_dt = jnp.pad(w_in_t[:, C_POOL:C_POOL + 2 * H], ((0, 0), (0, DTW - 2 * H), (0, 0)))
    pad_heads = lambda a: jnp.pad(a.reshape(DEPTH, 1, 2 * H), ((0, 0), (0, 0), (0, DTW - 2 * H)))
    dtb = pad_heads(dt_bias)
    alog_r = pad_heads(a_log)
    alog_c = alog_r.reshape(DEPTH, DTW, 1)
    dskip_x = jnp.repeat(d_skip, P, axis=2)
    norm_w = ssd_norm_w.reshape(DEPTH, 1, DI)
    gm = g_mix.reshape(DEPTH, 1, D)
    gf = g_ffn.reshape(DEPTH, 1, D)
    cbias = conv_b.reshape(DEPTH, 1, -1)
    wso = w_ssd_out.astype(BF)
    pw = pool_w.astype(BF)
    psc = pool_scale.reshape(DEPTH, 1, D)
    wpo = w_pool_out.astype(BF)
    wo = w_out.astype(BF)
    wgu = w_gate_up.astype(BF)
    wdn = w_down.astype(BF)
    gfin = g_final.reshape(1, D)

    xl = x.reshape(nb * L, D)
    xc_ = ctx.reshape(nb * Lc, D)
    tm_l = IN_TM
    tm_f = FFN_TM
    for l in range(DEPTH):
        last = l == DEPTH - 1

        def prepare(x2, row_fn, tm, seq):
            main, dt, dtt = _in_proj(x2, gm, mod3, l, row_fn, w_main, w_dt, dtb, tm)
            xcv = _conv(main, shift, conv_w, cbias, l, nb, seq, transpose=False)
            bt = _conv(main, shift, conv_w, cbias, l, nb, seq, transpose=True)
            return main, dt, dtt, xcv, bt

        def mix_and_ffn(x2, parts, ent, row_fn_mix, row_fn_ffn, seq, seg, final):
            main, dt, dtt, xcv, bt = parts
            y = _ssd_out(xcv, bt, dt, dtt, ent[0], ent[1], alog_r, alog_c, dskip_x, l, nb, seq)
            x2 = _mixer_out(y, main, x2, mod3, l, row_fn_mix, seg, norm_w, wso, pw, psc, wpo, wo)
            return _ffn(x2, gf, mod3, l, row_fn_ffn, wgu, wdn, gfin, tm_f, final)

        cparts = prepare(xc_, lambda i: ctx_row, min(IN_TM, nb * Lc), Lc)
        centf, centb, cfin = _ssd_state(cparts[3], cparts[4], cparts[1], None, alog_r, expand2, l, nb, Lc)
        lparts = prepare(xl, lambda i: i // (L // tm_l), tm_l, L)
        lentf, lentb, _ = _ssd_state(lparts[3], lparts[4], lparts[1], cfin, alog_r, expand2, l, nb, L)
        xl = mix_and_ffn(xl, lparts, (lentf, lentb), lambda i: i // (L // (TM_MIX * MIX_SUB)),
                         lambda i: i // (L // tm_f), L, GRID_W, last)
        if not last:
            xc_ = mix_and_ffn(xc_, cparts, (centf, centb), lambda i: ctx_row, lambda i: ctx_row, Lc, Lc, False)
    return xl.reshape(nb, L, D)
```

```python
import functools

import numpy as np
import jax
import jax.numpy as jnp
from jax import lax
from jax.experimental import pallas as pl
from jax.experimental.pallas import tpu as pltpu

F32 = jnp.float32
BF = jnp.bfloat16

D = 1024
DEPTH = 2
H = 32
P = 64
G = 8
R = H // G
N = 128
Q = 128
DI = H * P
GN = G * N
GW = R * P
KC = 5
FH = 2816
POOL_WINDOWS = (2, 4, 8, 16)
NPW = len(POOL_WINDOWS)
PG = D // NPW
GRID_W = 64
EPS = 1e-6
LOG2E = 1.4426950408889634
DTW = 128

C_Z, C_X, C_B, C_C, C_POOL, C_GS, C_GP = 0, 2048, 4096, 5120, 6144, 7168, 8192
MAIN_W = 9216

LANES = 128
VMEM_LIMIT = 56 * 1024 * 1024

ADALN_TN = 1536
IN_TM, IN_TN = 1024, 3072
CONV_TL, CONV_TC = 1024, 1024
FFN_TM = 512


def _dot(a, b):
    return jnp.dot(a, b, preferred_element_type=F32)


def _dot_nt(a, b):
    return lax.dot_general(a, b, (((1,), (1,)), ((), ())), preferred_element_type=F32)


def _dot_tn(a, b):
    return lax.dot_general(a, b, (((0,), (0,)), ((), ())), preferred_element_type=F32)


def _split2(a):
    hi = a.astype(BF)
    return hi, (a - hi.astype(F32)).astype(BF)


def _split3(a):
    hi = a.astype(BF)
    r = a - hi.astype(F32)
    mid = r.astype(BF)
    lo = (r - mid.astype(F32)).astype(BF)
    return hi, mid, lo


def _dot3_rhs(m, a):
    hi, mid, lo = _split3(a)
    return (_dot(m, lo) + _dot(m, mid)) + _dot(m, hi)


def _dot3_lhs(a, m):
    hi, mid, lo = _split3(a)
    return (_dot(lo, m) + _dot(mid, m)) + _dot(hi, m)


def _sigmoid(v):
    return 0.5 * jnp.tanh(0.5 * v) + 0.5


def _silu(v):
    hv = 0.5 * v
    return hv * jnp.tanh(hv) + hv


def _params(sem):
    return pltpu.CompilerParams(dimension_semantics=sem, vmem_limit_bytes=VMEM_LIMIT)


def _mod_spec(l, k, row_fn):
    return pl.BlockSpec((None, 1, D), lambda *ids: ((l * 8 + row_fn(*ids)) * 6 + k, 0, 0))


def _layer_spec(shape, l):
    zeros = (0,) * len(shape)
    return pl.BlockSpec((None,) + tuple(shape), lambda *ids: (l,) + zeros)


def _adaln_kernel(c_ref, w_ref, b_ref, o_ref):
    s_hi, s_lo = _split2(_silu(c_ref[...]))
    w_hi, w_lo = _split2(w_ref[...])
    acc = (_dot(s_lo, w_hi) + _dot(s_hi, w_lo)) + _dot(s_hi, w_hi)
    o_ref[...] = acc + b_ref[...]


def _adaln(cond8, w_ada, b_ada):
    tn = ADALN_TN
    return pl.pallas_call(
        _adaln_kernel,
        grid=(DEPTH, 6 * D // tn),
        in_specs=[
            pl.BlockSpec((8, D), lambda l, j: (0, 0)),
            pl.BlockSpec((None, D, tn), lambda l, j: (l, 0, j)),
            pl.BlockSpec((None, 1, tn), lambda l, j: (l, 0, j)),
        ],
        out_specs=pl.BlockSpec((None, 8, tn), lambda l, j: (l, 0, j)),
        out_shape=jax.ShapeDtypeStruct((DEPTH, 8, 6 * D), F32),
        compiler_params=_params(("arbitrary", "arbitrary")),
        name="adaln",
    )(cond8, w_ada, b_ada.reshape(DEPTH, 1, 6 * D))


def _inproj_kernel(x_ref, xn_ref, g_ref, sh_ref, sc_ref, shn_ref, scn_ref, w_ref, wdt_ref, dtb_ref,
                   o_ref, dt_ref, dtt_ref, h_ref, dts_ref, *, slab, nj):
    i = pl.program_id(0)
    j = pl.program_id(1)
    slot = i % 2
    tm, tn = o_ref.shape
    wh, wl = _split2(wdt_ref[...])
    wcat = jnp.concatenate([wh, wl], axis=0)

    def prologue(x, sh, sc):
        y = x * lax.rsqrt(jnp.mean(x * x, axis=-1, keepdims=True) + EPS)
        h = (y * g_ref[...]) * (1.0 + sc) + sh
        hb, hl = _split2(h)
        p = _dot_nt(hb, wcat)
        d = (_dot_nt(hl, wh) + p[:, DTW:]) + p[:, :DTW] + dtb_ref[...]
        return hb, jnp.maximum(d, 0.0) + jnp.log1p(jnp.exp(-jnp.abs(d)))

    @pl.when((i == 0) & (j == 0))
    def _():
        hb, dt = prologue(x_ref[...], sh_ref[...], sc_ref[...])
        h_ref[0] = hb
        dts_ref[0] = dt

    @pl.when(j == 0)
    def _():
        dt = dts_ref[slot]
        dt_ref[...] = dt
        dtt_ref[...] = dt.T

    def step(jj):
        rows = slice(min(jj * slab, tm - slab), min(jj * slab, tm - slab) + slab)
        hb, dt = prologue(xn_ref[rows, :], shn_ref[...], scn_ref[...])
        h_ref[1 - slot, rows, :] = hb
        dts_ref[1 - slot, rows, :] = dt
        v = _dot_nt(h_ref[slot], w_ref[0])
        lo, hi = jj * tn, (jj + 1) * tn
        n_silu = min(max(C_X - lo, 0), tn)
        n_sig = min(max(hi - C_GS, 0), tn)
        if n_silu:
            o_ref[:, :n_silu] = _silu(v[:, :n_silu]).astype(BF)
        if tn - n_silu - n_sig:
            o_ref[:, n_silu:tn - n_sig] = v[:, n_silu:tn - n_sig].astype(BF)
        if n_sig:
            o_ref[:, tn - n_sig:] = _sigmoid(v[:, tn - n_sig:]).astype(BF)

    for jj in range(nj):
        pl.when(j == jj)(functools.partial(step, jj))


def _in_proj(x2, g, mod3, l, row_fn, w_main, w_dt, dt_bias, tm):
    T = x2.shape[0]
    n = T // tm
    tn = IN_TN
    nj = MAIN_W // tn
    slab = -(-tm // (nj * Q)) * Q
    nlow = C_POOL // tn
    w_row = lambda j: pl.multiple_of(jnp.where(j < nlow, j * tn, j * tn + 2 * H), 2 * H)
    nxt = lambda i: jnp.minimum(i + 1, n - 1)
    return pl.pallas_call(
        functools.partial(_inproj_kernel, slab=slab, nj=nj),
        grid=(n, nj),
        in_specs=[
            pl.BlockSpec((tm, D), lambda i, j: (0, 0)),
            pl.BlockSpec((tm, D), lambda i, j: (nxt(i), 0)),
            _layer_spec((1, D), l),
            _mod_spec(l, 0, lambda i, j: row_fn(0)),
            _mod_spec(l, 1, lambda i, j: row_fn(0)),
            _mod_spec(l, 0, lambda i, j: row_fn(nxt(i))),
            _mod_spec(l, 1, lambda i, j: row_fn(nxt(i))),
            pl.BlockSpec((pl.Element(1), pl.Element(tn), pl.Element(D)), lambda i, j: (l, w_row(j), 0)),
            _layer_spec((DTW, D), l),
            _layer_spec((1, DTW), l),
        ],
        out_specs=[
            pl.BlockSpec((tm, tn), lambda i, j: (i, j)),
            pl.BlockSpec((tm, DTW), lambda i, j: (i, 0)),
            pl.BlockSpec((DTW, tm), lambda i, j: (0, i)),
        ],
        out_shape=[
            jax.ShapeDtypeStruct((T, MAIN_W), BF),
            jax.ShapeDtypeStruct((T, DTW), F32),
            jax.ShapeDtypeStruct((DTW, T), F32),
        ],
        scratch_shapes=[pltpu.VMEM((2, tm, D), BF), pltpu.VMEM((2, tm, DTW), F32)],
        compiler_params=_params(("arbitrary", "arbitrary")),
        name="in_proj",
    )(x2, x2, g, mod3, mod3, mod3, mod3, w_main, w_dt, dt_bias)


CH = 64
SUBL = 8
CONV_SUB = 256


def _shift_table():
    s = np.zeros((Q // SUBL, KC - 1, SUBL, 2 * Q), np.float32)
    t = np.arange(Q)
    for si, k in enumerate([k for k in range(KC) if k != KC // 2]):
        s[t // SUBL, si, t % SUBL, CH + t + k - KC // 2] = 1.0
    return s.reshape((KC - 1) * Q, 2 * Q)


def _conv_kernel(prev_ref, cur_ref, next_ref, s_ref, w_ref, b_ref, o_ref, ext_ref, *, tl, nl, transpose):
    i = pl.program_id(1)
    prev = prev_ref[...]
    nxt = next_ref[...]
    ext_ref[0:CH, :] = jnp.where(i > 0, prev, jnp.zeros_like(prev))
    ext_ref[CH:CH + tl, :] = cur_ref[...]
    ext_ref[CH + tl:2 * CH + tl, :] = jnp.where(i < nl - 1, nxt, jnp.zeros_like(nxt))
    tc = cur_ref.shape[1]
    for r in range(tl // Q):
        for c0 in range(0, tc, CONV_SUB):
            cols = slice(c0, c0 + CONV_SUB)
            win = ext_ref[r * Q:(r + 2) * Q, cols]
            sh = _dot(s_ref[...], win)
            hv = 0.5 * b_ref[:, cols]
            si = 0
            for k in range(KC):
                if k == KC // 2:
                    tap = win[CH:CH + Q, :].astype(F32)
                else:
                    starts = [(tg * (KC - 1) + si) * SUBL for tg in range(Q // SUBL)]
                    tap = jnp.concatenate([sh[a:a + SUBL, :] for a in starts], axis=0)
                    si += 1
                hv = hv + (0.5 * w_ref[k:k + 1, cols]) * tap
            out = hv * jnp.tanh(hv) + hv
            if transpose:
                o_ref[cols, r * Q:(r + 1) * Q] = out.T.astype(BF)
            else:
                o_ref[r * Q:(r + 1) * Q, cols] = out.astype(BF)


def _conv(main, shift, conv_w, conv_b, l, nb, L, transpose):
    T = nb * L
    tc = CONV_TC
    tl = min(L, CONV_TL)
    nl = L // tl
    rb = tl // CH
    last = T // CH - 1
    if transpose:
        ncol = GN // tc
        in_cb = lambda j: j + C_B // tc
        w_cb = lambda j: j + (C_B - C_X) // tc
        out_spec = pl.BlockSpec((tl, tc), lambda b, i, j: (b * nl + i, j))
        out_shape = jax.ShapeDtypeStruct((T, GN), BF)
    else:
        nx = DI // tc
        ncol = (DI + GN) // tc
        in_cb = lambda j: jnp.where(j < nx, j + C_X // tc, j - nx + C_C // tc)
        w_cb = lambda j: jnp.where(j < nx, j, j - nx + (C_C - C_X) // tc)
        out_spec = pl.BlockSpec((tl, tc), lambda b, i, j: (b * nl + i, j))
        out_shape = jax.ShapeDtypeStruct((T, DI + GN), BF)
    return pl.pallas_call(
        functools.partial(_conv_kernel, tl=tl, nl=nl, transpose=False),
        grid=(nb, nl, ncol),
        in_specs=[
            pl.BlockSpec((CH, tc), lambda b, i, j: (jnp.maximum((b * nl + i) * rb - 1, 0), in_cb(j))),
            pl.BlockSpec((tl, tc), lambda b, i, j: (b * nl + i, in_cb(j))),
            pl.BlockSpec((CH, tc), lambda b, i, j: (jnp.minimum((b * nl + i + 1) * rb, last), in_cb(j))),
            pl.BlockSpec(((KC - 1) * Q, 2 * Q), lambda b, i, j: (0, 0)),
            pl.BlockSpec((None, KC, tc), lambda b, i, j: (l, 0, w_cb(j))),
            pl.BlockSpec((None, 1, tc), lambda b, i, j: (l, 0, w_cb(j))),
        ],
        out_specs=out_spec,
        out_shape=out_shape,
        scratch_shapes=[pltpu.VMEM((tl + 2 * CH, tc), BF)],
        compiler_params=_params(("arbitrary", "arbitrary", "arbitrary")),
        name="conv_t" if transpose else "conv",
    )(main, main, main, shift, conv_w, conv_b)


def _tri(kind):
    r = lax.broadcasted_iota(jnp.int32, (Q, Q), 0)
    c = lax.broadcasted_iota(jnp.int32, (Q, Q), 1)
    return (c <= r) if kind == "le" else (c >= r)


def _expand_table():
    e = np.zeros((DTW, 2 * DI), np.float32)
    for k in range(2 * H):
        e[k, k * P:(k + 1) * P] = 1.0
    return np.concatenate([e, e], axis=0)


XROWS = 16


def _expand_lhs(w, dec):
    w_hi, w_lo = _split2(w)
    d_hi, d_mid, d_lo = [t.astype(F32) for t in _split3(jnp.broadcast_to(dec, (XROWS, DTW)))]
    rid = lax.broadcasted_iota(jnp.int32, (XROWS, DTW), 0)
    extra = jnp.where(rid == 0, d_hi, jnp.where(rid == 1, d_mid, jnp.where(rid == 2, d_lo, 0.0))).astype(BF)
    zero = jnp.zeros((XROWS, DTW), BF)
    return jnp.concatenate([jnp.concatenate([w_hi, w_lo], axis=1), jnp.concatenate([extra, zero], axis=1)], axis=0)


MAX_CPS = 4


def _ssd_state_kernel(*refs, nc, has_init):
    CPS = refs[-1].shape[0]
    if has_init:
        (xf_ref, btf_ref, dtf_ref, xb_ref, btb_ref, dtb_ref, init_ref, alr_ref, e2_ref,
         entf_ref, entb_ref, fin_ref, st_ref, xw_ref, dec_ref) = refs
    else:
        (xf_ref, btf_ref, dtf_ref, xb_ref, btb_ref, dtb_ref, alr_ref, e2_ref,
         entf_ref, entb_ref, fin_ref, st_ref, xw_ref, dec_ref) = refs
    c = pl.program_id(1)

    @pl.when(c == 0)
    def _():
        st_ref[...] = init_ref[...] if has_init else jnp.zeros_like(st_ref)

    a_row = -jnp.exp(alr_ref[...])
    dirs = ((xf_ref, btf_ref, dtf_ref, entf_ref), (xb_ref, btb_ref, dtb_ref, entb_ref))
    order = [[(s, s) for s in range(CPS)], [(s, CPS - 1 - s) for s in range(CPS)]]
    for d, (x_ref, bt_ref, dt_ref, ent_ref) in enumerate(dirs):
        for s, ck in order[d]:
            rows = slice(ck * Q, (ck + 1) * Q)
            dt = dt_ref[rows, :]
            acum = _dot3_rhs(_tri("le" if d == 0 else "ge").astype(BF), dt * a_row)
            tot = acum[Q - 1:Q] if d == 0 else acum[0:1]
            ex = _dot(_expand_lhs(dt * jnp.exp(tot - acum), jnp.exp(tot)), e2_ref[:, d * DI:(d + 1) * DI])
            dec_ref[s, d] = (ex[Q + 2:Q + 3] + ex[Q + 1:Q + 2]) + ex[Q:Q + 1]
            xw_ref[s, d] = x_ref[rows, :] * ex[0:Q].astype(BF)
    for s in range(CPS):
        for d, (x_ref, bt_ref, dt_ref, ent_ref) in enumerate(dirs):
            ck = order[d][s][1]
            for g in range(G):
                cols = slice(g * GW, (g + 1) * GW)
                upd = _dot_tn(bt_ref[ck * Q:(ck + 1) * Q, g * N:(g + 1) * N], xw_ref[s, d, :, cols])
                st = st_ref[d, g]
                ent_ref[ck, g] = st.astype(BF)
                st_ref[d, g] = st * dec_ref[s, d, :, cols] + upd

    @pl.when(c == nc - 1)
    def _():
        fin_ref[...] = st_ref[...]


def _ssd_state(xc, bt, dt, init, alog_r, expand2, l, nb, L):
    CPS = min(MAX_CPS, L // Q)
    nc = L // (CPS * Q)
    fw = lambda b, c: b * nc + c
    bw = lambda b, c: b * nc + (nc - 1 - c)
    has_init = init is not None
    chunk_specs = lambda ch: [
        pl.BlockSpec((CPS * Q, DI), lambda b, c: (ch(b, c), 0)),
        pl.BlockSpec((CPS * Q, GN), lambda b, c: (ch(b, c), 0)),
        pl.BlockSpec((CPS * Q, DTW), lambda b, c: (ch(b, c), 0)),
    ]
    st_spec = pl.BlockSpec((None, 2, G, N, GW), lambda b, c: (b, 0, 0, 0, 0))
    in_specs = chunk_specs(fw) + chunk_specs(bw) + ([st_spec] if has_init else []) + [
        _layer_spec((1, DTW), l),
        pl.BlockSpec((2 * DTW, 2 * DI), lambda b, c: (0, 0)),
    ]
    args = (xc, bt, dt, xc, bt, dt) + ((init,) if has_init else ()) + (alog_r, expand2)
    return pl.pallas_call(
        functools.partial(_ssd_state_kernel, nc=nc, has_init=has_init),
        grid=(nb, nc),
        in_specs=in_specs,
        out_specs=[
            pl.BlockSpec((None, CPS, G, N, GW), lambda b, c: (b, c, 0, 0, 0)),
            pl.BlockSpec((None, CPS, G, N, GW), lambda b, c: (b, nc - 1 - c, 0, 0, 0)),
            st_spec,
        ],
        out_shape=[
            jax.ShapeDtypeStruct((nb, nc * CPS, G, N, GW), BF),
            jax.ShapeDtypeStruct((nb, nc * CPS, G, N, GW), BF),
            jax.ShapeDtypeStruct((nb, 2, G, N, GW), F32),
        ],
        scratch_shapes=[pltpu.VMEM((2, G, N, GW), F32), pltpu.VMEM((CPS, 2, Q, DI), BF),
                        pltpu.VMEM((CPS, 2, 1, DI), F32)],
        compiler_params=_params(("arbitrary", "arbitrary")),
        name="ssd_state",
    )(*args)


OUT_CPS = 4


def _fill_blockdiag(bd_ref, xg):
    blk = lax.broadcasted_iota(jnp.int32, (Q, GW), 1) // P
    for r in range(R):
        bd_ref[r * Q:(r + 1) * Q, :] = jnp.where(blk == r, xg, jnp.zeros_like(xg))


def _ssd_out_kernel(xc_ref, bt_ref, dt_ref, dtt_ref, entf_ref, entb_ref,
                    alr_ref, alc_ref, dsk_ref, o_ref, bd_ref):
    x_ref = xc_ref.at[:, 0:DI]
    c_ref = xc_ref.at[:, DI:DI + GN]
    a_row = -jnp.exp(alr_ref[...])
    a_col = -jnp.exp(alc_ref[...])
    left = lax.broadcasted_iota(jnp.int32, (Q, 2 * P), 1) < P
    mask = [_tri("le"), _tri("ge")]
    ri = lax.broadcasted_iota(jnp.int32, (Q, Q), 0)
    ci = lax.broadcasted_iota(jnp.int32, (Q, Q), 1)
    below, above = ci < ri, ci > ri
    ents = (entf_ref, entb_ref)
    dsum = dsk_ref[0:1, :] + dsk_ref[1:2, :]
    for s in range(o_ref.shape[0] // Q):
        rows = slice(s * Q, (s + 1) * Q)
        dt = dt_ref[rows, :]
        dtt = dtt_ref[:, rows]
        log_dtt = jnp.log(dtt)
        rowd = jnp.log(dtt[0:H] + dtt[H:2 * H]) * LOG2E
        acum, rowv = [], []
        for d in range(2):
            acum.append(_dot3_rhs(mask[d].astype(BF), dt * a_row) * LOG2E)
            acum_t = _dot3_lhs(dtt * a_col, mask[1 - d].astype(BF))
            rowv.append((log_dtt - acum_t) * LOG2E)
        for g in range(G):
            cols = slice(g * GW, (g + 1) * GW)
            xg = x_ref[rows, cols]
            cg = c_ref[rows, g * N:(g + 1) * N]
            cbb = _dot_nt(cg, bt_ref[rows, g * N:(g + 1) * N]).astype(BF)
            _fill_blockdiag(bd_ref.at[s, g], xg)
            yin = [_dot(cg, ents[d][s, g]) for d in range(2)]
            ms, yi = [], []
            for r0 in range(0, R, 2):
                es = ([], [])
                for h in (g * R + r0, g * R + r0 + 1):
                    acols = [jnp.broadcast_to(acum[d][:, d * H + h:d * H + h + 1], (Q, Q)) for d in range(2)]
                    expo = jnp.where(below, acols[0] + rowv[0][h:h + 1, :],
                                     jnp.where(above, acols[1] + rowv[1][H + h:H + h + 1, :], rowd[h:h + 1, :]))
                    ms.append(cbb * jnp.exp2(expo).astype(BF))
                    for d in range(2):
                        es[d].append(jnp.exp2(acols[d]))
                pc = slice(r0 * P, (r0 + 2) * P)
                yi.append(yin[0][:, pc] * jnp.where(left, es[0][0], es[0][1])
                          + yin[1][:, pc] * jnp.where(left, es[1][0], es[1][1]))
            y = dsum[:, cols] * xg.astype(F32) + _dot(jnp.concatenate(ms, axis=1), bd_ref[s, g])
            o_ref[rows, cols] = (y + jnp.concatenate(yi, axis=1)).astype(BF)


def _ssd_out(xc, bt, dt, dtt, entf, entb, alog_r, alog_c, dskip_x, l, nb, L):
    cps = min(OUT_CPS, L // Q)
    rows = cps * Q
    nc = L // rows
    T = nb * L
    ch = lambda b, c: b * nc + c
    ent_spec = pl.BlockSpec((None, cps, G, N, GW), lambda b, c: (b, c, 0, 0, 0))
    return pl.pallas_call(
        _ssd_out_kernel,
        grid=(nb, nc),
        in_specs=[
            pl.BlockSpec((rows, DI + GN), lambda b, c: (ch(b, c), 0)),
            pl.BlockSpec((rows, GN), lambda b, c: (ch(b, c), 0)),
            pl.BlockSpec((rows, DTW), lambda b, c: (ch(b, c), 0)),
            pl.BlockSpec((DTW, rows), lambda b, c: (0, ch(b, c))),
            ent_spec,
            ent_spec,
            _layer_spec((1, DTW), l),
            _layer_spec((DTW, 1), l),
            _layer_spec((2, DI), l),
        ],
        out_specs=pl.BlockSpec((rows, DI), lambda b, c: (ch(b, c), 0)),
        out_shape=jax.ShapeDtypeStruct((T, DI), BF),
        scratch_shapes=[pltpu.VMEM((cps, G, R * Q, GW), BF)],
        compiler_params=_params(("parallel", "parallel")),
        name="ssd_out",
    )(xc, bt, dt, dtt, entf, entb, alog_r, alog_c, dskip_x)


TM_MIX = 256
MIX_SUB = 4


def _mixer_kernel(y_ref, z_ref, pgg_ref, x_ref, ga_ref, nw_ref, band_ref, cnt_ref, wso_ref,
                  pw_ref, psc_ref, wpo_ref, wo_ref, o_ref):
    u_ref = pgg_ref.at[:, 0:D]
    gs_ref = pgg_ref.at[:, C_GS - C_POOL:C_GS - C_POOL + D]
    gp_ref = pgg_ref.at[:, C_GP - C_POOL:C_GP - C_POOL + D]
    for s in range(MIX_SUB):
        rows = slice(s * TM_MIX, (s + 1) * TM_MIX)
        yns = []
        for g in range(G):
            cols = slice(g * GW, (g + 1) * GW)
            yz = (y_ref[rows, cols] * z_ref[rows, cols]).astype(F32)
            yn = yz * lax.rsqrt(jnp.mean(yz * yz, axis=-1, keepdims=True) + EPS)
            yns.append((yn * nw_ref[:, cols]).astype(BF))
        o_ssd = _dot(jnp.concatenate(yns, axis=1), wso_ref[...])
        pms = []
        for gi in range(NPW):
            ug = u_ref[rows, gi * PG:(gi + 1) * PG]
            wsum = _dot(band_ref[gi], ug)
            pm = wsum / cnt_ref[:, gi:gi + 1] - ug.astype(F32)
            pms.append(_dot(pm.astype(BF), pw_ref[gi]))
        pmc = (jnp.concatenate(pms, axis=1) * psc_ref[...]).astype(BF)
        o_pool = _dot(pmc, wpo_ref[...])
        mix = gs_ref[rows, :].astype(F32) * o_ssd + gp_ref[rows, :].astype(F32) * o_pool
        o_ref[rows, :] = x_ref[rows, :] + ga_ref[...] * _dot(mix.astype(BF), wo_ref[...])


def _pool_tables(seg):
    t = np.arange(TM_MIX)
    same = (t[:, None] // seg) == (t[None, :] // seg)
    ts = t % seg
    bands, cnts = [], []
    for k in POOL_WINDOWS:
        lo = t[:, None] - k // 2
        hi = t[:, None] + k // 2
        bands.append(same & (t[None, :] >= lo) & (t[None, :] < hi))
        cnts.append(np.minimum(ts + k // 2, seg) - np.maximum(ts - k // 2, 0))
    cnt = np.ones((TM_MIX, LANES), np.float32)
    cnt[:, :NPW] = np.stack(cnts, axis=1)
    return jnp.asarray(np.stack(bands).astype(np.float32), BF), jnp.asarray(cnt)


def _mixer_out(y, main, x2, mod3, l, row_fn, seg, norm_w, w_ssd_out, pool_w, pool_scale, w_pool_out, w_out):
    T = x2.shape[0]
    tm = TM_MIX * MIX_SUB
    band, cnt = _pool_tables(seg)
    return pl.pallas_call(
        _mixer_kernel,
        grid=(T // tm,),
        in_specs=[
            pl.BlockSpec((tm, DI), lambda i: (i, 0)),
            pl.BlockSpec((tm, DI), lambda i: (i, C_Z // DI)),
            pl.BlockSpec((tm, MAIN_W - C_POOL), lambda i: (i, C_POOL // (MAIN_W - C_POOL))),
            pl.BlockSpec((tm, D), lambda i: (i, 0)),
            _mod_spec(l, 2, row_fn),
            _layer_spec((1, DI), l),
            pl.BlockSpec((NPW, TM_MIX, TM_MIX), lambda i: (0, 0, 0)),
            pl.BlockSpec((TM_MIX, LANES), lambda i: (0, 0)),
            _layer_spec((DI, D), l),
            _layer_spec((NPW, PG, PG), l),
            _layer_spec((1, D), l),
            _layer_spec((D, D), l),
            _layer_spec((D, D), l),
        ],
        out_specs=pl.BlockSpec((tm, D), lambda i: (i, 0)),
        out_shape=jax.ShapeDtypeStruct((T, D), F32),
        compiler_params=_params(("parallel",)),
        name="mixer_out",
    )(y, main, main, x2, mod3, norm_w, band, cnt, w_ssd_out, pool_w, pool_scale, w_pool_out, w_out)


FFN_CHUNKS = (768, 768, 768, 512)


def _ffn_kernel(x_ref, xn_ref, g_ref, sh_ref, sc_ref, ga_ref, shn_ref, scn_ref, wg_ref, wu_ref, wd_ref, gf_ref,
                o_ref, h_ref, *, final):
    i = pl.program_id(0)
    slot = i % 2

    def norm_mod(x, sh, sc):
        y = x * lax.rsqrt(jnp.mean(x * x, axis=-1, keepdims=True) + EPS)
        return ((y * g_ref[...]) * (1.0 + sc) + sh).astype(BF)

    @pl.when(i == 0)
    def _():
        h_ref[0] = norm_mod(x_ref[...], sh_ref[...], sc_ref[...])

    h_ref[1 - slot] = norm_mod(xn_ref[...], shn_ref[...], scn_ref[...])
    h = h_ref[slot]
    acc = None
    c0 = 0
    for cw in FFN_CHUNKS:
        act = (_silu(_dot(h, wg_ref[:, c0:c0 + cw])) * _dot(h, wu_ref[:, c0:c0 + cw])).astype(BF)
        part = _dot(act, wd_ref[c0:c0 + cw, :])
        acc = part if acc is None else acc + part
        c0 += cw
    xn = x_ref[...] + ga_ref[...] * acc
    if final:
        xn = (xn * lax.rsqrt(jnp.mean(xn * xn, axis=-1, keepdims=True) + EPS)) * gf_ref[...]
    o_ref[...] = xn


def _ffn(x2, g, mod3, l, row_fn, w_gate_up, w_down, g_final, tm, final):
    assert sum(FFN_CHUNKS) == FH
    T = x2.shape[0]
    n = T // tm
    nxt = lambda i: jnp.minimum(i + 1, n - 1)
    once = pl.Buffered(1)
    return pl.pallas_call(
        functools.partial(_ffn_kernel, final=final),
        grid=(n,),
        in_specs=[
            pl.BlockSpec((tm, D), lambda i: (i, 0)),
            pl.BlockSpec((tm, D), lambda i: (nxt(i), 0)),
            _layer_spec((1, D), l),
            _mod_spec(l, 3, row_fn),
            _mod_spec(l, 4, row_fn),
            _mod_spec(l, 5, row_fn),
            _mod_spec(l, 3, lambda i: row_fn(nxt(i))),
            _mod_spec(l, 4, lambda i: row_fn(nxt(i))),
            pl.BlockSpec((None, D, FH), lambda i: (l, 0, 0), pipeline_mode=once),
            pl.BlockSpec((None, D, FH), lambda i: (l, 0, 1), pipeline_mode=once),
            pl.BlockSpec((None, FH, D), lambda i: (l, 0, 0), pipeline_mode=once),
            pl.BlockSpec((1, D), lambda i: (0, 0)),
        ],
        out_specs=pl.BlockSpec((tm, D), lambda i: (i, 0)),
        out_shape=jax.ShapeDtypeStruct((T, D), F32),
        scratch_shapes=[pltpu.VMEM((2, tm, D), BF)],
        compiler_params=_params(("arbitrary",)),
        name="ffn",
    )(x2, x2, g, mod3, mod3, mod3, mod3, mod3, w_gate_up, w_gate_up, w_down, g_final)


def kernel(x, c, ctx, c_ctx, w_ada, b_ada, g_mix, w_in, conv_w, conv_b, dt_bias, a_log, d_skip, ssd_norm_w,
           w_ssd_out, pool_w, pool_scale, w_pool_out, w_out, g_ffn, w_gate_up, w_down, g_final):
    nb, L, _ = x.shape
    Lc = ctx.shape[1]
    ctx_row = nb

    cond8 = jnp.zeros((8, D), F32).at[:nb].set(c).at[ctx_row].set(c_ctx)
    mod3 = _adaln(cond8, w_ada, b_ada).reshape(DEPTH * 8 * 6, 1, D)

    shift = jnp.asarray(_shift_table(), BF)
    expand2 = jnp.asarray(_expand_table(), BF)

    w_in_t = jnp.swapaxes(w_in, 1, 2)
    w_main = w_in_t.astype(BF)
    w_dt = jnp.pad(w_in_t[:, C_POOL:C_POOL + 2 * H], ((0, 0), (0, DTW - 2 * H), (0, 0)))
    pad_heads = lambda a: jnp.pad(a.reshape(DEPTH, 1, 2 * H), ((0, 0), (0, 0), (0, DTW - 2 * H)))
    dtb = pad_heads(dt_bias)
    alog_r = pad_heads(a_log)
    alog_c = alog_r.reshape(DEPTH, DTW, 1)
    dskip_x = jnp.repeat(d_skip, P, axis=2)
    norm_w = ssd_norm_w.reshape(DEPTH, 1, DI)
    gm = g_mix.reshape(DEPTH, 1, D)
    gf = g_ffn.reshape(DEPTH, 1, D)
    cbias = conv_b.reshape(DEPTH, 1, -1)
    wso = w_ssd_out.astype(BF)
    pw = pool_w.astype(BF)
    psc = pool_scale.reshape(DEPTH, 1, D)
    wpo = w_pool_out.astype(BF)
    wo = w_out.astype(BF)
    wgu = w_gate_up.astype(BF)
    wdn = w_down.astype(BF)
    gfin = g_final.reshape(1, D)

    xl = x.reshape(nb * L, D)
    xc_ = ctx.reshape(nb * Lc, D)
    tm_l = IN_TM
    tm_f = FFN_TM
    for l in range(DEPTH):
        last = l == DEPTH - 1

        def prepare(x2, row_fn, tm, seq):
            main, dt, dtt = _in_proj(x2, gm, mod3, l, row_fn, w_main, w_dt, dtb, tm)
            xcv = _conv(main, shift, conv_w, cbias, l, nb, seq, transpose=False)
            bt = _conv(main, shift, conv_w, cbias, l, nb, seq, transpose=True)
            return main, dt, dtt, xcv, bt

        def mix_and_ffn(x2, parts, ent, row_fn_mix, row_fn_ffn, seq, seg, final):
            main, dt, dtt, xcv, bt = parts
            y = _ssd_out(xcv, bt, dt, dtt, ent[0], ent[1], alog_r, alog_c, dskip_x, l, nb, seq)
            x2 = _mixer_out(y, main, x2, mod3, l, row_fn_mix, seg, norm_w, wso, pw, psc, wpo, wo)
            return _ffn(x2, gf, mod3, l, row_fn_ffn, wgu, wdn, gfin, tm_f, final)

        cparts = prepare(xc_, lambda i: ctx_row, min(IN_TM, nb * Lc), Lc)
        centf, centb, cfin = _ssd_state(cparts[3], cparts[4], cparts[1], None, alog_r, expand2, l, nb, Lc)
        lparts = prepare(xl, lambda i: i // (L // tm_l), tm_l, L)
        lentf, lentb, _ = _ssd_state(lparts[3], lparts[4], lparts[1], cfin, alog_r, expand2, l, nb, L)
        xl = mix_and_ffn(xl, lparts, (lentf, lentb), lambda i: i // (L // (TM_MIX * MIX_SUB)),
                         lambda i: i // (L // tm_f), L, GRID_W, last)
        if not last:
            xc_ = mix_and_ffn(xc_, cparts, (centf, centb), lambda i: ctx_row, lambda i: ctx_row, Lc, Lc, False)
    return xl.reshape(nb, L, D)
```
